```python
import math
import jax, jax.numpy as jnp
from jax import lax
import numpy as np

D_MODEL = 2048
BATCH = 1
SEQ = 16384
DEPTH = 2
DEC_BATCH = 4
DEC_SEQ = 2048
PAST_LEN = 128

N_MIXERS = 2
N_ATTN_LAYERS = (DEPTH + 1) // 2
N_SSD_LAYERS = DEPTH // 2
EPS = 1e-5

DA_HEADS = 16
DA_HEAD_DIM = D_MODEL // DA_HEADS // 2
DA_V_DIM = 2 * DA_HEAD_DIM
ROPE_THETA = 10000.0
Q_BLOCK = 128

SSD_D_INNER = 2 * D_MODEL
SSD_HEAD_DIM = 64
SSD_HEADS = SSD_D_INNER // SSD_HEAD_DIM
SSD_GROUPS = 8
SSD_HPG = SSD_HEADS // SSD_GROUPS
SSD_D_STATE = 128
SSD_D_CONV = 4
SSD_CHUNK = 128
SSD_CONV_DIM = SSD_D_INNER + 2 * SSD_GROUPS * SSD_D_STATE
SSD_IN_DIM = SSD_D_INNER + SSD_CONV_DIM + 2 * SSD_HEADS

D_FF = -(-8 * D_MODEL // 768) * 256

kernel_name = 'bidir_diffattn_ssd_hybrid'


def rmsnorm(x, w):
    x32 = x.astype(jnp.float32)
    y = x32 * lax.rsqrt(jnp.mean(x32 * x32, axis=-1, keepdims=True) + EPS)
    return (y * w.astype(jnp.float32)).astype(x.dtype)


def rope_tables(seq):
    inv = 1.0 / (ROPE_THETA ** (jnp.arange(0, DA_HEAD_DIM, 2, dtype=jnp.float32) / DA_HEAD_DIM))
    ang = jnp.arange(seq, dtype=jnp.float32)[:, None] * inv[None, :]
    return jnp.cos(ang), jnp.sin(ang)


def apply_rope(x, cos, sin):
    x1, x2 = jnp.split(x.astype(jnp.float32), 2, axis=-1)
    c = cos[None, :, None, None, :]
    s = sin[None, :, None, None, :]
    return jnp.concatenate([x1 * c - x2 * s, x2 * c + x1 * s], axis=-1).astype(x.dtype)


def diff_attention(x, w_qkv, w_o, lq1, lk1, lq2, lk2, subln_w, lambda_init):
    b, s, _ = x.shape
    f32 = jnp.float32
    q, k, v = jnp.split(x @ w_qkv, 3, axis=-1)
    q = q.reshape(b, s, DA_HEADS, 2, DA_HEAD_DIM)
    k = k.reshape(b, s, DA_HEADS, 2, DA_HEAD_DIM)
    v = v.reshape(b, s, DA_HEADS, DA_V_DIM)
    cos, sin = rope_tables(s)
    q = apply_rope(q, cos, sin) * (DA_HEAD_DIM ** -0.5)
    k = apply_rope(k, cos, sin)
    lam = (jnp.exp(jnp.sum(lq1.astype(f32) * lk1.astype(f32)))
           - jnp.exp(jnp.sum(lq2.astype(f32) * lk2.astype(f32))) + lambda_init)
    nb = s // Q_BLOCK
    q_blocks = jnp.moveaxis(q.reshape(b, nb, Q_BLOCK, DA_HEADS, 2, DA_HEAD_DIM), 1, 0)

    def block(qb):
        sc = jnp.einsum('bqhtd,bkhtd->bhtqk', qb, k, preferred_element_type=f32)
        p = jax.nn.softmax(sc, axis=-1)
        a = p[:, :, 0] - lam * p[:, :, 1]
        return jnp.einsum('bhqk,bkhe->bqhe', a.astype(v.dtype), v)

    o = jnp.moveaxis(lax.map(block, q_blocks), 0, 1).reshape(b, s, DA_HEADS, DA_V_DIM)
    o = rmsnorm(o, subln_w) * (1.0 - lambda_init)
    return o.reshape(b, s, DA_HEADS * DA_V_DIM) @ w_o


def segsum(a):
    q = a.shape[-1]
    xr = jnp.broadcast_to(a[..., :, None], a.shape + (q,))
    strict = jnp.tril(jnp.ones((q, q), dtype=bool), -1)
    seg = jnp.cumsum(jnp.where(strict, xr, 0.0), axis=-2)
    return jnp.where(jnp.tril(jnp.ones((q, q), dtype=bool)), seg, -jnp.inf)


def ssd_scan(x, dt, a, bm, cm):
    b, s = x.shape[:2]
    nc = s // SSD_CHUNK

    def chunks(t):
        return jnp.moveaxis(t.reshape((b, nc, SSD_CHUNK) + t.shape[2:]), 1, 0)

    def step(state, inp):
        xq, dtq, bq, cq = inp
        da = jnp.moveaxis(dtq * a, 1, -1)
        cum = jnp.cumsum(da, axis=-1)
        L = jnp.exp(segsum(da))
        xdt = xq * dtq[..., None]
        cb = jnp.einsum('blgn,bsgn->bgls', cq, bq)
        y_diag = jnp.einsum('bgls,bgrls,bsgrp->blgrp', cb, L, xdt)
        y_off = jnp.einsum('blgn,bgrpn,bgrl->blgrp', cq, state, jnp.exp(cum))
        new_state = (state * jnp.exp(cum[..., -1])[..., None, None]
                     + jnp.einsum('bsgn,bgrs,bsgrp->bgrpn', bq, L[..., -1, :], xdt))
        return new_state, y_diag + y_off

    init = jnp.zeros((b, SSD_GROUPS, SSD_HPG, SSD_HEAD_DIM, SSD_D_STATE), jnp.float32)
    _, ys = lax.scan(step, init, (chunks(x), chunks(dt), chunks(bm), chunks(cm)))
    return jnp.moveaxis(ys, 0, 1).reshape(x.shape)


def flip_seq(t):
    return jnp.flip(t, axis=1)


def ssd_mixer(x, w_in, conv_w, conv_b, dt_bias, a_log, d_skip, norm_w, w_out):
    b, s, _ = x.shape
    f32 = jnp.float32
    z, xbc, dt = jnp.split(x @ w_in, [SSD_D_INNER, SSD_D_INNER + SSD_CONV_DIM], axis=-1)
    xbc = lax.conv_general_dilated(
        xbc, conv_w[:, None, :].astype(xbc.dtype), window_strides=(1,),
        padding=[((SSD_D_CONV - 1) // 2, SSD_D_CONV // 2)],
        dimension_numbers=('NWC', 'WIO', 'NWC'), feature_group_count=SSD_CONV_DIM)
    xbc = jax.nn.silu((xbc + conv_b).astype(f32))
    xs, bm, cm = jnp.split(xbc, [SSD_D_INNER, SSD_D_INNER + SSD_GROUPS * SSD_D_STATE], axis=-1)
    xs = xs.reshape(b, s, SSD_GROUPS, SSD_HPG, SSD_HEAD_DIM)
    bm = bm.reshape(b, s, SSD_GROUPS, SSD_D_STATE)
    cm = cm.reshape(b, s, SSD_GROUPS, SSD_D_STATE)
    dt = jax.nn.softplus(dt.astype(f32).reshape(b, s, 2, SSD_HEADS) + dt_bias.astype(f32))
    dt = dt.reshape(b, s, 2, SSD_GROUPS, SSD_HPG)
    a = -jnp.exp(a_log.astype(f32)).reshape(2, SSD_GROUPS, SSD_HPG)
    y_fwd = ssd_scan(xs, dt[:, :, 0], a[0], bm, cm)
    y_bwd = flip_seq(ssd_scan(flip_seq(xs), flip_seq(dt[:, :, 1]), a[1], flip_seq(bm), flip_seq(cm)))
    y = y_fwd + y_bwd + d_skip.astype(f32).reshape(SSD_GROUPS, SSD_HPG)[:, :, None] * xs
    y = y.reshape(b, s, SSD_D_INNER) * jax.nn.silu(z.astype(f32))
    y = rmsnorm(y.reshape(b, s, SSD_GROUPS, SSD_D_INNER // SSD_GROUPS),
                norm_w.reshape(SSD_GROUPS, SSD_D_INNER // SSD_GROUPS))
    return y.reshape(b, s, SSD_D_INNER).astype(x.dtype) @ w_out


def swiglu(x, w_gu, w_down):
    g, u = jnp.split(x @ w_gu, 2, axis=-1)
    return (jax.nn.silu(g) * u) @ w_down


def trunk(x, norm_mix, norm_ffn, norm_final, da_w_qkv, da_w_o, da_lambda_q1, da_lambda_k1,
          da_lambda_q2, da_lambda_k2, da_subln, ssd_w_in, ssd_conv_w, ssd_conv_b, ssd_dt_bias,
          ssd_a_log, ssd_d, ssd_norm, ssd_w_out, ffn_w_gu, ffn_w_down):
    for i in range(DEPTH):
        h = rmsnorm(x, norm_mix[i])
        j = i // N_MIXERS
        if i % N_MIXERS == 0:
            lambda_init = 0.8 - 0.6 * math.exp(-0.3 * i)
            x = x + diff_attention(h, da_w_qkv[j], da_w_o[j], da_lambda_q1[j], da_lambda_k1[j],
                                   da_lambda_q2[j], da_lambda_k2[j], da_subln[j], lambda_init)
        else:
            x = x + ssd_mixer(h, ssd_w_in[j], ssd_conv_w[j], ssd_conv_b[j], ssd_dt_bias[j],
                              ssd_a_log[j], ssd_d[j], ssd_norm[j], ssd_w_out[j])
        x = x + swiglu(rmsnorm(x, norm_ffn[i]), ffn_w_gu[i], ffn_w_down[i])
    return rmsnorm(x, norm_final)


def setup_inputs(seed: int = 0) -> dict:
    key = jax.random.key(seed)
    ks = jax.random.split(key, 24)
    f32 = jnp.float32

    def nrm(k, shape, scale):
        return jax.random.normal(k, shape, f32) * scale

    na, ns = N_ATTN_LAYERS, N_SSD_LAYERS
    dt0 = jnp.exp(jax.random.uniform(ks[14], (ns, 2, SSD_HEADS), f32, math.log(1e-3), math.log(1e-1)))
    dt_bias = dt0 + jnp.log(-jnp.expm1(-dt0))
    return {
        'x_prompt': nrm(ks[0], (BATCH, SEQ, D_MODEL), 1.0),
        'x_sample': nrm(ks[1], (DEC_BATCH, DEC_SEQ, D_MODEL), 1.0),
        'norm_mix': 1.0 + nrm(ks[2], (DEPTH, D_MODEL), 0.01),
        'norm_ffn': 1.0 + nrm(ks[3], (DEPTH, D_MODEL), 0.01),
        'norm_final': 1.0 + nrm(ks[4], (D_MODEL,), 0.01),
        'da_w_qkv': nrm(ks[5], (na, D_MODEL, 3 * D_MODEL), D_MODEL ** -0.5),
        'da_w_o': nrm(ks[6], (na, D_MODEL, D_MODEL), D_MODEL ** -0.5),
        'da_lambda_q1': nrm(ks[7], (na, DA_HEAD_DIM), 0.1),
        'da_lambda_k1': nrm(ks[8], (na, DA_HEAD_DIM), 0.1),
        'da_lambda_q2': nrm(ks[9], (na, DA_HEAD_DIM), 0.1),
        'da_lambda_k2': nrm(ks[10], (na, DA_HEAD_DIM), 0.1),
        'da_subln': 1.0 + nrm(ks[11], (na, DA_V_DIM), 0.01),
        'ssd_w_in': nrm(ks[12], (ns, D_MODEL, SSD_IN_DIM), D_MODEL ** -0.5),
        'ssd_conv_w': nrm(ks[13], (ns, SSD_D_CONV, SSD_CONV_DIM), SSD_D_CONV ** -0.5),
        'ssd_conv_b': nrm(ks[15], (ns, SSD_CONV_DIM), 0.02),
        'ssd_dt_bias': dt_bias,
        'ssd_a_log': jnp.log(jax.random.uniform(ks[16], (ns, 2, SSD_HEADS), f32, 1.0, 16.0)),
        'ssd_d': 1.0 + nrm(ks[17], (ns, SSD_HEADS), 0.01),
        'ssd_norm': 1.0 + nrm(ks[18], (ns, SSD_D_INNER), 0.01),
        'ssd_w_out': nrm(ks[19], (ns, SSD_D_INNER, D_MODEL), SSD_D_INNER ** -0.5),
        'ffn_w_gu': nrm(ks[20], (DEPTH, D_MODEL, 2 * D_FF), D_MODEL ** -0.5),
        'ffn_w_down': nrm(ks[21], (DEPTH, D_FF, D_MODEL), D_FF ** -0.5),
    }


def reference(x_prompt, x_sample, norm_mix, norm_ffn, norm_final, da_w_qkv, da_w_o,
              da_lambda_q1, da_lambda_k1, da_lambda_q2, da_lambda_k2, da_subln,
              ssd_w_in, ssd_conv_w, ssd_conv_b, ssd_dt_bias, ssd_a_log, ssd_d, ssd_norm,
              ssd_w_out, ffn_w_gu, ffn_w_down):
    y_prompt = trunk(x_prompt, norm_mix, norm_ffn, norm_final, da_w_qkv, da_w_o,
                     da_lambda_q1, da_lambda_k1, da_lambda_q2, da_lambda_k2, da_subln,
                     ssd_w_in, ssd_conv_w, ssd_conv_b, ssd_dt_bias, ssd_a_log, ssd_d,
                     ssd_norm, ssd_w_out, ffn_w_gu, ffn_w_down)
    y_sample = trunk(x_sample, norm_mix, norm_ffn, norm_final, da_w_qkv, da_w_o,
                     da_lambda_q1, da_lambda_k1, da_lambda_q2, da_lambda_k2, da_subln,
                     ssd_w_in, ssd_conv_w, ssd_conv_b, ssd_dt_bias, ssd_a_log, ssd_d,
                     ssd_norm, ssd_w_out, ffn_w_gu, ffn_w_down)
    return (y_prompt, y_sample)
```

```python
import functools
import math

import numpy as np
import jax
import jax.numpy as jnp
from jax import lax
from jax.experimental import pallas as pl
from jax.experimental.pallas import tpu as pltpu

F32 = jnp.float32
BF16 = jnp.bfloat16

EPS = 1e-5
LANES = 128
SUBLANES = 8
VMEM_LIMIT = 56 * 1024 * 1024

DA_HEADS = 16
DA_HEAD_DIM = 64
DA_V_DIM = 2 * DA_HEAD_DIM
ROPE_THETA = 10000.0

SSD_HEAD_DIM = 64
SSD_GROUPS = 8
SSD_HPG = 8
SSD_HEADS = SSD_GROUPS * SSD_HPG
SSD_D_STATE = 128
SSD_D_INNER = SSD_HEADS * SSD_HEAD_DIM
SSD_CHUNK = 128
SSD_GROUP_W = SSD_HPG * SSD_HEAD_DIM
SSD_BC_W = SSD_GROUPS * SSD_D_STATE
SSD_CONV_DIM = SSD_D_INNER + 2 * SSD_BC_W


def _cparams(n_axes):
    return pltpu.CompilerParams(dimension_semantics=("arbitrary",) * n_axes,
                                vmem_limit_bytes=VMEM_LIMIT)


def _rms(x, w):
    ms = jnp.mean(x * x, axis=-1, keepdims=True)
    return x * lax.rsqrt(ms + EPS) * w


def _silu(x):
    return x / (1.0 + jnp.exp(-x))


def _pick(n, pref):
    b = min(n, pref)
    while n % b:
        b //= 2
    return b


def _qkv_kernel(x_ref, nw_ref, w_ref, cos_ref, sin_ref, o_ref, h_ref, *, n_q_tiles, n_rope_tiles):
    j = pl.program_id(1)

    @pl.when(j == 0)
    def _():
        h_ref[...] = _rms(x_ref[...], nw_ref[...]).astype(BF16)

    y = jnp.dot(h_ref[...], w_ref[...], preferred_element_type=F32)

    @pl.when(j < n_rope_tiles)
    def _():
        bn = y.shape[1]
        reps = bn // LANES
        c = jnp.tile(cos_ref[...], (1, reps))
        s = jnp.tile(sin_ref[...], (1, reps))
        lane = lax.broadcasted_iota(jnp.int32, y.shape, 1)
        first_half = (lane % DA_HEAD_DIM) < (DA_HEAD_DIM // 2)
        partner = jnp.where(first_half, pltpu.roll(y, bn - DA_HEAD_DIM // 2, 1),
                            pltpu.roll(y, DA_HEAD_DIM // 2, 1))
        scale = jnp.where(j < n_q_tiles, DA_HEAD_DIM ** -0.5, 1.0)
        o_ref[...] = ((y * c + partner * s) * scale).astype(o_ref.dtype)

    @pl.when(j >= n_rope_tiles)
    def _():
        o_ref[...] = y.astype(o_ref.dtype)


def _qkv_proj(x, nw, w, cos_t, sin_t):
    t, d = x.shape
    n = w.shape[1]
    bm = _pick(t, 1024)
    bn = _pick(d, 512)
    return pl.pallas_call(
        functools.partial(_qkv_kernel, n_q_tiles=d // bn, n_rope_tiles=2 * d // bn),
        grid=(t // bm, n // bn),
        in_specs=[
            pl.BlockSpec((bm, d), lambda i, j: (i, 0)),
            pl.BlockSpec((1, d), lambda i, j: (0, 0)),
            pl.BlockSpec((d, bn), lambda i, j: (0, j)),
            pl.BlockSpec((bm, LANES), lambda i, j: (i, 0)),
            pl.BlockSpec((bm, LANES), lambda i, j: (i, 0)),
        ],
        out_specs=pl.BlockSpec((bm, bn), lambda i, j: (i, j)),
        out_shape=jax.ShapeDtypeStruct((t, n), BF16),
        scratch_shapes=[pltpu.VMEM((bm, d), BF16)],
        compiler_params=_cparams(2),
        name="qkv_proj",
    )(x, nw, w, cos_t, sin_t)


def _attn_kernel(lam_ref, sub_ref, q_ref, k_ref, v_ref, o_ref, *, tk, lambda_init):
    q = q_ref[...]
    tq = q.shape[0]
    lane = lax.broadcasted_iota(jnp.int32, q.shape, 1)
    zero = jnp.zeros_like(q)
    q_maps = (jnp.where(lane < DA_HEAD_DIM, q, zero), jnp.where(lane >= DA_HEAD_DIM, q, zero))
    nk = k_ref.shape[0] // tk

    def body(i, carry):
        start = pl.multiple_of(i * tk, tk)
        k = k_ref[pl.ds(start, tk), :]
        v = v_ref[pl.ds(start, tk), :]
        out = []
        for qm, (m, l, acc) in zip(q_maps, carry):
            s = lax.dot_general(qm, k, (((1,), (1,)), ((), ())), preferred_element_type=F32)
            m_new = jnp.maximum(m, jnp.max(s, axis=-1, keepdims=True))
            alpha = jnp.exp(m - m_new)
            p = jnp.exp(s - m_new)
            l = alpha * l + jnp.sum(p, axis=-1, keepdims=True)
            acc = alpha * acc + jnp.dot(p.astype(BF16), v, preferred_element_type=F32)
            out.append((m_new, l, acc))
        return tuple(out)

    init = tuple((jnp.full((tq, 1), -jnp.inf, F32), jnp.zeros((tq, 1), F32),
                  jnp.zeros((tq, DA_V_DIM), F32)) for _ in range(2))
    (_, l1, a1), (_, l2, a2) = lax.fori_loop(0, nk, body, init)

    lam_p = lam_ref[...]
    t1 = jnp.sum(lam_p[0:1, :] * lam_p[1:2, :], axis=-1, keepdims=True)
    t2 = jnp.sum(lam_p[2:3, :] * lam_p[3:4, :], axis=-1, keepdims=True)
    lam = jnp.exp(t1) - jnp.exp(t2) + lambda_init
    o = a1 / l1 - lam * (a2 / l2)
    o_ref[...] = (_rms(o, sub_ref[...]) * (1.0 - lambda_init)).astype(o_ref.dtype)


def _attention_group(qkv, lam_p, sub_w, row0, nseq, s, lambda_init):
    tq = _pick(s, 512)
    tk = _pick(s, 512)
    nq = s // tq
    kern = functools.partial(_attn_kernel, tk=tk, lambda_init=lambda_init)
    return pl.pallas_call(
        kern,
        grid=(nseq, DA_HEADS, nq),
        in_specs=[
            pl.BlockSpec((4, DA_HEAD_DIM), lambda b, h, i: (0, 0)),
            pl.BlockSpec((1, DA_V_DIM), lambda b, h, i: (0, 0)),
            pl.BlockSpec((tq, DA_V_DIM), lambda b, h, i: (row0 // tq + b * nq + i, h)),
            pl.BlockSpec((s, DA_V_DIM), lambda b, h, i: (row0 // s + b, DA_HEADS + h)),
            pl.BlockSpec((s, DA_V_DIM), lambda b, h, i: (row0 // s + b, 2 * DA_HEADS + h)),
        ],
        out_specs=pl.BlockSpec((tq, DA_V_DIM), lambda b, h, i: (b * nq + i, h)),
        out_shape=jax.ShapeDtypeStruct((nseq * s, DA_HEADS * DA_V_DIM), BF16),
        compiler_params=_cparams(3),
        name="diff_attention",
    )(lam_p, sub_w, qkv, qkv, qkv)


def _matmul_res_kernel(a_ref, w_ref, r_ref, o_ref):
    o_ref[...] = r_ref[...] + jnp.dot(a_ref[...], w_ref[...], preferred_element_type=F32)


def _matmul_res(a, w, res):
    t, k = a.shape
    n = w.shape[1]
    bm = _pick(t, 1024)
    bn = _pick(n, 512)
    return pl.pallas_call(
        _matmul_res_kernel,
        grid=(t // bm, n // bn),
        in_specs=[
            pl.BlockSpec((bm, k), lambda i, j: (i, 0)),
            pl.BlockSpec((k, bn), lambda i, j: (0, j)),
            pl.BlockSpec((bm, bn), lambda i, j: (i, j)),
        ],
        out_specs=pl.BlockSpec((bm, bn), lambda i, j: (i, j)),
        out_shape=jax.ShapeDtypeStruct((t, n), F32),
        compiler_params=_cparams(2),
        name="matmul_residual",
    )(a, w, res)


def _ffn_kernel(x_ref, nw_ref, wg_ref, wu_ref, wd_ref, nf_ref, o_ref, h_ref, acc_ref, *, final_norm):
    j = pl.program_id(1)

    @pl.when(j == 0)
    def _():
        h_ref[...] = _rms(x_ref[...], nw_ref[...]).astype(BF16)
        acc_ref[...] = jnp.zeros_like(acc_ref)

    h = h_ref[...]
    g = jnp.dot(h, wg_ref[...], preferred_element_type=F32)
    u = jnp.dot(h, wu_ref[...], preferred_element_type=F32)
    a = (_silu(g) * u).astype(BF16)
    acc_ref[...] += jnp.dot(a, wd_ref[...], preferred_element_type=F32)

    @pl.when(j == pl.num_programs(1) - 1)
    def _():
        r = acc_ref[...] + x_ref[...]
        if final_norm:
            r = _rms(r, nf_ref[...])
        o_ref[...] = r


def _ffn(x, nw, w_gu, w_down, nf, final_norm):
    t, d = x.shape
    dff = w_down.shape[0]
    bm = _pick(t, 512)
    bf = _pick(dff, 512)
    nf_tiles = dff // bf
    return pl.pallas_call(
        functools.partial(_ffn_kernel, final_norm=final_norm),
        grid=(t // bm, nf_tiles),
        in_specs=[
            pl.BlockSpec((bm, d), lambda i, j: (i, 0)),
            pl.BlockSpec((1, d), lambda i, j: (0, 0)),
            pl.BlockSpec((d, bf), lambda i, j: (0, j)),
            pl.BlockSpec((d, bf), lambda i, j: (0, j + nf_tiles)),
            pl.BlockSpec((bf, d), lambda i, j: (j, 0)),
            pl.BlockSpec((1, d), lambda i, j: (0, 0)),
        ],
        out_specs=pl.BlockSpec((bm, d), lambda i, j: (i, 0)),
        out_shape=jax.ShapeDtypeStruct((t, d), F32),
        scratch_shapes=[pltpu.VMEM((bm, d), BF16), pltpu.VMEM((bm, d), F32)],
        compiler_params=_cparams(2),
        name="swiglu_ffn",
    )(x, nw, w_gu, w_gu, w_down, nf)


def _ssd_in_kernel(x_ref, nw_ref, w_ref, wdt_ref, o_ref, dt_ref, h_ref):
    j = pl.program_id(1)

    @pl.when(j == 0)
    def _():
        h = _rms(x_ref[...], nw_ref[...]).astype(BF16)
        h_ref[...] = h
        dt_ref[...] = jnp.dot(h, wdt_ref[...], preferred_element_type=F32)

    o_ref[...] = jnp.dot(h_ref[...], w_ref[...], preferred_element_type=F32).astype(o_ref.dtype)


def _ssd_in_proj(x, nw, w_zx, w_dt):
    t, d = x.shape
    n = w_zx.shape[1]
    ndt = w_dt.shape[1]
    bm = _pick(t, 1024)
    bn = _pick(n, 512)
    return pl.pallas_call(
        _ssd_in_kernel,
        grid=(t // bm, n // bn),
        in_specs=[
            pl.BlockSpec((bm, d), lambda i, j: (i, 0)),
            pl.BlockSpec((1, d), lambda i, j: (0, 0)),
            pl.BlockSpec((d, bn), lambda i, j: (0, j)),
            pl.BlockSpec((d, ndt), lambda i, j: (0, 0)),
        ],
        out_specs=[
            pl.BlockSpec((bm, bn), lambda i, j: (i, j)),
            pl.BlockSpec((bm, ndt), lambda i, j: (i, 0)),
        ],
        out_shape=[jax.ShapeDtypeStruct((t, n), BF16), jax.ShapeDtypeStruct((t, ndt), F32)],
        scratch_shapes=[pltpu.VMEM((bm, d), BF16)],
        compiler_params=_cparams(2),
        name="ssd_in_proj",
    )(x, nw, w_zx, w_dt)


def _conv_kernel(flags_ref, prev_ref, cur_ref, next_ref, w_ref, b_ref, o_ref):
    f = flags_ref[pl.program_id(0)]
    at_start = (f & 1) == 1
    at_end = (f & 2) == 2
    x = cur_ref[...].astype(F32)
    bt = x.shape[0]
    row = lax.broadcasted_iota(jnp.int32, x.shape, 0)
    before = jnp.where(at_start, 0.0, prev_ref[SUBLANES - 1:SUBLANES, :].astype(F32))
    after = jnp.where(at_end, 0.0, next_ref[0:2, :].astype(F32))
    xm1 = jnp.where(row == 0, before, pltpu.roll(x, 1, 0))
    xp1 = jnp.where(row == bt - 1, after[0:1], pltpu.roll(x, bt - 1, 0))
    xp2 = jnp.where(row == bt - 2, after[0:1],
                    jnp.where(row == bt - 1, after[1:2], pltpu.roll(x, bt - 2, 0)))
    w = w_ref[...]
    y = xm1 * w[0:1] + x * w[1:2] + xp1 * w[2:3] + xp2 * w[3:4] + b_ref[...]
    o_ref[...] = _silu(y).astype(o_ref.dtype)


def _conv_silu(zx, conv_w, conv_b, seqs, col0):
    t = zx.shape[0]
    ncol = conv_w.shape[1]
    bt = _pick(math.gcd(*[s for _, s in seqs]), 512)
    bc = _pick(ncol, 1024)
    nblk = t // bt
    starts = {r for r, _ in seqs}
    ends = {r + s for r, s in seqs}
    flags = np.array([(1 if i * bt in starts else 0) | (2 if (i + 1) * bt in ends else 0)
                      for i in range(nblk)], np.int32)
    hb = bt // SUBLANES
    c0 = col0 // bc
    grid_spec = pltpu.PrefetchScalarGridSpec(
        num_scalar_prefetch=1,
        grid=(nblk, ncol // bc),
        in_specs=[
            pl.BlockSpec((SUBLANES, bc), lambda i, j, f: (jnp.maximum(i * hb - 1, 0), c0 + j)),
            pl.BlockSpec((bt, bc), lambda i, j, f: (i, c0 + j)),
            pl.BlockSpec((SUBLANES, bc), lambda i, j, f: (jnp.minimum((i + 1) * hb, t // SUBLANES - 1), c0 + j)),
            pl.BlockSpec((4, bc), lambda i, j, f: (0, j)),
            pl.BlockSpec((1, bc), lambda i, j, f: (0, j)),
        ],
        out_specs=pl.BlockSpec((bt, bc), lambda i, j, f: (i, j)),
    )
    return pl.pallas_call(
        _conv_kernel,
        grid_spec=grid_spec,
        out_shape=jax.ShapeDtypeStruct((t, ncol), BF16),
        compiler_params=_cparams(2),
        name="ssd_conv_silu",
    )(jnp.asarray(flags), zx, zx, zx, conv_w, conv_b)


def _pair_cols(mat, h0):
    rows = mat.shape[0]
    lane = lax.broadcasted_iota(jnp.int32, (rows, LANES), 1)
    a = jnp.broadcast_to(mat[:, h0:h0 + 1], (rows, LANES))
    b = jnp.broadcast_to(mat[:, h0 + 1:h0 + 2], (rows, LANES))
    return jnp.where(lane < SSD_HEAD_DIM, a, b)


def _group_cols(mat, h0):
    return jnp.concatenate([_pair_cols(mat, h0 + 2 * j) for j in range(SSD_HPG // 2)], axis=1)


def _scan_kernel(flags_ref, xs_ref, b_ref, c_ref, dt_ref, bias_ref, alog_ref, d_ref, *rest,
                 reverse):
    if reverse:
        yin_ref, y_ref, state_ref, ut_ref = rest
    else:
        y_ref, state_ref, ut_ref = rest
    q = SSD_CHUNK
    dir_off = SSD_HEADS if reverse else 0

    @pl.when(flags_ref[pl.program_id(0)] == 1)
    def _():
        state_ref[...] = jnp.zeros_like(state_ref)

    xdt_in = dt_ref[...] + bias_ref[...]
    dt = jnp.maximum(xdt_in, 0.0) + jnp.log(1.0 + jnp.exp(-jnp.abs(xdt_in)))
    da = dt * (-jnp.exp(alog_ref[...]))
    row = lax.broadcasted_iota(jnp.int32, (q, LANES), 0)
    cum = da
    k = 1
    while k < q:
        cum = cum + jnp.where(row >= k, pltpu.roll(cum, k, 0), 0.0)
        k *= 2
    tot = cum[q - 1:q, :]
    if reverse:
        u = da - cum
        scale_off = jnp.exp(tot + u)
        decay_end = jnp.exp(-u)
    else:
        u = cum
        scale_off = jnp.exp(u)
        decay_end = jnp.exp(tot - u)
    state_decay = jnp.exp(tot)
    ut_ref[...] = u.T

    li = lax.broadcasted_iota(jnp.int32, (q, q), 0)
    si = lax.broadcasted_iota(jnp.int32, (q, q), 1)
    allowed = (si >= li) if reverse else (li >= si)
    lane = lax.broadcasted_iota(jnp.int32, (q, LANES), 1)
    low = lane < SSD_HEAD_DIM

    for g in range(SSD_GROUPS):
        h0 = dir_off + g * SSD_HPG
        cols = slice(g * SSD_GROUP_W, (g + 1) * SSD_GROUP_W)
        xs = xs_ref[:, cols].astype(F32)
        bg = b_ref[:, g * SSD_D_STATE:(g + 1) * SSD_D_STATE]
        cg = c_ref[:, g * SSD_D_STATE:(g + 1) * SSD_D_STATE]
        cb = lax.dot_general(cg, bg, (((1,), (1,)), ((), ())), preferred_element_type=F32)
        xdt = xs * _group_cols(dt, h0)
        xdt_b = xdt.astype(BF16)
        xw = (xdt * _group_cols(decay_end, h0)).astype(BF16)
        st = state_ref[g]
        y = jnp.dot(cg, st.astype(BF16), preferred_element_type=F32) * _group_cols(scale_off, h0)
        state_ref[g] = (st * _group_cols(state_decay, h0)
                        + lax.dot_general(bg, xw, (((0,), (0,)), ((), ())), preferred_element_type=F32))
        tiles = []
        for j in range(SSD_HPG // 2):
            att = []
            for r in (2 * j, 2 * j + 1):
                h = h0 + r
                diff = u[:, h:h + 1] - ut_ref[h:h + 1, :]
                att.append((cb * jnp.exp(jnp.where(allowed, diff, -jnp.inf))).astype(BF16))
            xt = xdt_b[:, j * LANES:(j + 1) * LANES]
            zt = jnp.zeros_like(xt)
            rhs = jnp.concatenate([jnp.where(low, xt, zt), jnp.where(low, zt, xt)], axis=0)
            tiles.append(jnp.dot(jnp.concatenate(att, axis=1), rhs, preferred_element_type=F32))
        y = y + jnp.concatenate(tiles, axis=1)
        if reverse:
            y = y + yin_ref[:, cols].astype(F32)
        else:
            y = y + d_ref[:, cols] * xs
        y_ref[:, cols] = y.astype(y_ref.dtype)


def _ssd_scan(xbc, dt_raw, dt_bias, a_log, d_row, seqs, y_in):
    reverse = y_in is not None
    t = xbc.shape[0]
    q = SSD_CHUNK
    nchunk = t // q
    starts = {r // q for r, _ in seqs}
    ends = {(r + s) // q - 1 for r, s in seqs}
    order = list(range(nchunk))[::-1] if reverse else list(range(nchunk))
    flags = np.array([1 if c in (ends if reverse else starts) else 0 for c in order], np.int32)
    if reverse:
        blk = lambda c: nchunk - 1 - c
    else:
        blk = lambda c: c
    nxb = SSD_D_INNER // SSD_BC_W
    in_specs = [
        pl.BlockSpec((q, SSD_D_INNER), lambda c, f: (blk(c), 0)),
        pl.BlockSpec((q, SSD_BC_W), lambda c, f: (blk(c), nxb)),
        pl.BlockSpec((q, SSD_BC_W), lambda c, f: (blk(c), nxb + 1)),
        pl.BlockSpec((q, 2 * SSD_HEADS), lambda c, f: (blk(c), 0)),
        pl.BlockSpec((1, 2 * SSD_HEADS), lambda c, f: (0, 0)),
        pl.BlockSpec((1, 2 * SSD_HEADS), lambda c, f: (0, 0)),
        pl.BlockSpec((1, SSD_D_INNER), lambda c, f: (0, 0)),
    ]
    args = [jnp.asarray(flags), xbc, xbc, xbc, dt_raw, dt_bias, a_log, d_row]
    if reverse:
        in_specs.append(pl.BlockSpec((q, SSD_D_INNER), lambda c, f: (blk(c), 0)))
        args.append(y_in)
    grid_spec = pltpu.PrefetchScalarGridSpec(
        num_scalar_prefetch=1,
        grid=(nchunk,),
        in_specs=in_specs,
        out_specs=pl.BlockSpec((q, SSD_D_INNER), lambda c, f: (blk(c), 0)),
        scratch_shapes=[pltpu.VMEM((SSD_GROUPS, SSD_D_STATE, SSD_GROUP_W), F32),
                        pltpu.VMEM((2 * SSD_HEADS, q), F32)],
    )
    return pl.pallas_call(
        functools.partial(_scan_kernel, reverse=reverse),
        grid_spec=grid_spec,
        out_shape=jax.ShapeDtypeStruct((t, SSD_D_INNER), BF16),
        compiler_params=_cparams(1),
        name="ssd_scan_bwd" if reverse else "ssd_scan_fwd",
    )(*args)


def _ssd_out_kernel(y_ref, z_ref, nw_ref, w_ref, r_ref, o_ref, yn_ref):
    j = pl.program_id(1)

    @pl.when(j == 0)
    def _():
        for g in range(SSD_GROUPS):
            cols = slice(g * SSD_GROUP_W, (g + 1) * SSD_GROUP_W)
            gated = y_ref[:, cols].astype(F32) * _silu(z_ref[:, cols].astype(F32))
            yn_ref[:, cols] = _rms(gated, nw_ref[:, cols]).astype(BF16)

    o_ref[...] = r_ref[...] + jnp.dot(yn_ref[...], w_ref[...], preferred_element_type=F32)


def _ssd_out_proj(y, zx, nw, w, res):
    t, k = y.shape
    n = w.shape[1]
    bm = _pick(t, 512)
    bn = _pick(n, 512)
    return pl.pallas_call(
        _ssd_out_kernel,
        grid=(t // bm, n // bn),
        in_specs=[
            pl.BlockSpec((bm, k), lambda i, j: (i, 0)),
            pl.BlockSpec((bm, k), lambda i, j: (i, 0)),
            pl.BlockSpec((1, k), lambda i, j: (0, 0)),
            pl.BlockSpec((k, bn), lambda i, j: (0, j)),
            pl.BlockSpec((bm, bn), lambda i, j: (i, j)),
        ],
        out_specs=pl.BlockSpec((bm, bn), lambda i, j: (i, j)),
        out_shape=jax.ShapeDtypeStruct((t, n), F32),
        scratch_shapes=[pltpu.VMEM((bm, k), BF16)],
        compiler_params=_cparams(2),
        name="ssd_out_proj",
    )(y, zx, nw, w, res)


def _rope_tables(seqs, t):
    half = DA_HEAD_DIM // 2
    inv = 1.0 / (ROPE_THETA ** (jnp.arange(0, DA_HEAD_DIM, 2, dtype=F32) / DA_HEAD_DIM))
    pos = np.zeros((t,), np.float32)
    for r, s in seqs:
        pos[r:r + s] = np.arange(s, dtype=np.float32)
    ang = jnp.asarray(pos)[:, None] * inv[None, :]
    cos, sin = jnp.cos(ang), jnp.sin(ang)
    reps = LANES // DA_HEAD_DIM
    cos_t = jnp.tile(jnp.concatenate([cos, cos], axis=1), (1, reps))
    sin_t = jnp.tile(jnp.concatenate([-sin, sin], axis=1), (1, reps))
    assert cos_t.shape == (t, LANES) and half * 2 == DA_HEAD_DIM
    return cos_t, sin_t


def _seq_groups(seqs):
    groups = []
    for r, s in seqs:
        if groups and groups[-1][2] == s and groups[-1][0] + groups[-1][1] * s == r:
            groups[-1][1] += 1
        else:
            groups.append([r, 1, s])
    return [tuple(g) for g in groups]


def _attention_layer(x, seqs, nw, w_qkv, w_o, lam_p, sub_w, lambda_init):
    t = x.shape[0]
    cos_t, sin_t = _rope_tables(seqs, t)
    qkv = _qkv_proj(x, nw, w_qkv.astype(BF16), cos_t, sin_t)
    outs = [_attention_group(qkv, lam_p, sub_w, r, n, s, lambda_init) for r, n, s in _seq_groups(seqs)]
    o = outs[0] if len(outs) == 1 else jnp.concatenate(outs, axis=0)
    return _matmul_res(o, w_o.astype(BF16), x)


def _ssd_layer(x, seqs, nw, w_in, conv_w, conv_b, dt_bias, a_log, d_skip, norm_w, w_out):
    nzx = SSD_D_INNER + SSD_CONV_DIM
    zx, dt_raw = _ssd_in_proj(x, nw, w_in[:, :nzx].astype(BF16), w_in[:, nzx:].astype(BF16))
    xbc = _conv_silu(zx, conv_w, conv_b.reshape(1, -1), seqs, SSD_D_INNER)
    bias = dt_bias.reshape(1, -1)
    alog = a_log.reshape(1, -1)
    d_row = jnp.repeat(d_skip, SSD_HEAD_DIM).reshape(1, -1)
    y = _ssd_scan(xbc, dt_raw, bias, alog, d_row, seqs, None)
    y = _ssd_scan(xbc, dt_raw, bias, alog, d_row, seqs, y)
    return _ssd_out_proj(y, zx, norm_w.reshape(1, -1), w_out.astype(BF16), x)


def _trunk(x, seqs, norm_mix, norm_ffn, norm_final, da_w_qkv, da_w_o, da_lambda_q1, da_lambda_k1,
           da_lambda_q2, da_lambda_k2, da_subln, ssd_w_in, ssd_conv_w, ssd_conv_b, ssd_dt_bias,
           ssd_a_log, ssd_d, ssd_norm, ssd_w_out, ffn_w_gu, ffn_w_down):
    depth = norm_mix.shape[0]
    nf = norm_final.reshape(1, -1)
    for i in range(depth):
        j = i // 2
        nw = norm_mix[i].reshape(1, -1)
        if i % 2 == 0:
            lambda_init = 0.8 - 0.6 * math.exp(-0.3 * i)
            lam_p = jnp.stack([da_lambda_q1[j], da_lambda_k1[j], da_lambda_q2[j], da_lambda_k2[j]])
            x = _attention_layer(x, seqs, nw, da_w_qkv[j], da_w_o[j], lam_p,
                                 da_subln[j].reshape(1, -1), lambda_init)
        else:
            x = _ssd_layer(x, seqs, nw, ssd_w_in[j], ssd_conv_w[j], ssd_conv_b[j], ssd_dt_bias[j],
                           ssd_a_log[j], ssd_d[j], ssd_norm[j], ssd_w_out[j])
        x = _ffn(x, norm_ffn[i].reshape(1, -1), ffn_w_gu[i].astype(BF16), ffn_w_down[i].astype(BF16),
                 nf, final_norm=(i == depth - 1))
    return x


def kernel(x_prompt, x_sample, norm_mix, norm_ffn, norm_final, da_w_qkv, da_w_o, da_lambda_q1, da_lambda_k1, da_lambda_q2, da_lambda_k2, da_subln, ssd_w_in, ssd_conv_w, ssd_conv_b, ssd_dt_bias, ssd_a_log, ssd_d, ssd_norm, ssd_w_out, ffn_w_gu, ffn_w_down):
    d = x_prompt.shape[-1]
    seqs = []
    for arr in (x_prompt, x_sample):
        for _ in range(arr.shape[0]):
            seqs.append((sum(s for _, s in seqs), arr.shape[1]))
    x = jnp.concatenate([x_prompt.reshape(-1, d), x_sample.reshape(-1, d)], axis=0)
    y = _trunk(x, tuple(seqs), norm_mix, norm_ffn, norm_final, da_w_qkv, da_w_o, da_lambda_q1,
               da_lambda_k1, da_lambda_q2, da_lambda_k2, da_subln, ssd_w_in, ssd_conv_w, ssd_conv_b,
               ssd_dt_bias, ssd_a_log, ssd_d, ssd_norm, ssd_w_out, ffn_w_gu, ffn_w_down)
    n_p = x_prompt.shape[0] * x_prompt.shape[1]
    return y[:n_p].reshape(x_prompt.shape), y[n_p:].reshape(x_sample.shape)
```

```python
import functools
import math

import numpy as np
import jax
import jax.numpy as jnp
from jax import lax
from jax.experimental import pallas as pl
from jax.experimental.pallas import tpu as pltpu

F32 = jnp.float32
BF16 = jnp.bfloat16

EPS = 1e-5
LOG2E = 1.4426950408889634
LANES = 128
SUBLANES = 8
VMEM_LIMIT = 56 * 1024 * 1024

DA_HEADS = 16
DA_HEAD_DIM = 64
DA_V_DIM = 2 * DA_HEAD_DIM
ROPE_THETA = 10000.0

SSD_HEAD_DIM = 64
SSD_GROUPS = 8
SSD_HPG = 8
SSD_HEADS = SSD_GROUPS * SSD_HPG
SSD_D_STATE = 128
SSD_D_INNER = SSD_HEADS * SSD_HEAD_DIM
SSD_CHUNK = 128
SSD_GROUP_W = SSD_HPG * SSD_HEAD_DIM
SSD_BC_W = SSD_GROUPS * SSD_D_STATE
SSD_CONV_DIM = SSD_D_INNER + 2 * SSD_BC_W


def _cparams(n_axes):
    return pltpu.CompilerParams(dimension_semantics=("arbitrary",) * n_axes,
                                vmem_limit_bytes=VMEM_LIMIT)


def _rms(x, w):
    ms = jnp.mean(x * x, axis=-1, keepdims=True)
    return x * lax.rsqrt(ms + EPS) * w


def _silu(x):
    return x / (1.0 + jnp.exp(-x))


def _pick(n, pref):
    b = min(n, pref)
    while n % b:
        b //= 2
    return b


def _part_offsets(parts, bm):
    offs = [0]
    for p in parts:
        assert p.shape[0] % bm == 0
        offs.append(offs[-1] + p.shape[0] // bm)
    return offs


def _part_spec(block, offs, p, col_fn):
    lo, n = offs[p], offs[p + 1] - offs[p]
    return pl.BlockSpec(block, lambda i, j: (jnp.clip(i - lo, 0, n - 1), col_fn(i, j, lo, n)))


def _in_part(i, offs, p):
    return jnp.logical_and(i >= offs[p], i < offs[p + 1])


def _qkv_kernel(*refs, offs, n_q_tiles, n_rope_tiles):
    nparts = len(offs) - 1
    x_refs = refs[:nparts]
    nw_ref, w_ref, cos_ref, sin_ref, o_ref, h_ref = refs[nparts:]
    i = pl.program_id(0)
    j = pl.program_id(1)

    for p in range(nparts):
        @pl.when(jnp.logical_and(j == 0, _in_part(i, offs, p)))
        def _(p=p):
            h_ref[...] = _rms(x_refs[p][...], nw_ref[...]).astype(BF16)

    y = jnp.dot(h_ref[...], w_ref[...], preferred_element_type=F32)
    rope = j < n_rope_tiles
    c = jnp.where(rope, cos_ref[...], 1.0)
    s = jnp.where(rope, sin_ref[...], 0.0)
    scale = jnp.where(j < n_q_tiles, LOG2E * DA_HEAD_DIM ** -0.5, 1.0)
    c = c * scale
    s = s * scale
    for t in range(y.shape[1] // LANES):
        yt = y[:, t * LANES:(t + 1) * LANES]
        o_ref[:, t * LANES:(t + 1) * LANES] = (yt * c + pltpu.roll(yt, LANES // 2, 1) * s).astype(o_ref.dtype)


def _qkv_proj(x_parts, nw, w, cos_t, sin_t):
    d = x_parts[0].shape[1]
    t = sum(p.shape[0] for p in x_parts)
    n = w.shape[1]
    bm = _pick(math.gcd(*[p.shape[0] for p in x_parts]), 1024)
    bn = _pick(d, 512)
    offs = _part_offsets(x_parts, bm)
    return pl.pallas_call(
        functools.partial(_qkv_kernel, offs=tuple(offs), n_q_tiles=d // bn, n_rope_tiles=2 * d // bn),
        grid=(t // bm, n // bn),
        in_specs=[_part_spec((bm, d), offs, p, lambda i, j, lo, cnt: 0) for p in range(len(x_parts))] + [
            pl.BlockSpec((1, d), lambda i, j: (0, 0)),
            pl.BlockSpec((d, bn), lambda i, j: (0, j)),
            pl.BlockSpec((bm, LANES), lambda i, j: (i, 0)),
            pl.BlockSpec((bm, LANES), lambda i, j: (i, 0)),
        ],
        out_specs=pl.BlockSpec((bm, bn), lambda i, j: (i, j)),
        out_shape=jax.ShapeDtypeStruct((t, n), BF16),
        scratch_shapes=[pltpu.VMEM((bm, d), BF16)],
        compiler_params=_cparams(2),
        name="qkv_proj",
    )(*x_parts, nw, w, cos_t, sin_t)


def _attn_kernel(lam_ref, sub_ref, q_ref, k_ref, v_ref, o_ref, vt_ref, qt_ref, s_ref, smax_ref,
                 m_ref, l_ref, acc_ref, *, tk, lambda_init):
    tq = q_ref.shape[0]
    nk = k_ref.shape[0] // tk

    @pl.when(pl.program_id(2) == 0)
    def _():
        def transpose_chunk(c, carry):
            start = pl.multiple_of(c * tk, tk)
            vt_ref[:, pl.ds(start, tk)] = v_ref[pl.ds(start, tk), :].astype(F32).T.astype(BF16)
            return carry
        lax.fori_loop(0, nk, transpose_chunk, 0)

    qt = q_ref[...].astype(F32).T
    row = lax.broadcasted_iota(jnp.int32, qt.shape, 0)
    in_map0 = (row % DA_HEAD_DIM) < (DA_HEAD_DIM // 2)
    qt_ref[0] = jnp.where(in_map0, qt, 0.0).astype(BF16)
    qt_ref[1] = jnp.where(in_map0, 0.0, qt).astype(BF16)
    m_ref[...] = jnp.full(m_ref.shape, -jnp.inf, F32)
    l_ref[...] = jnp.zeros(l_ref.shape, F32)
    acc_ref[...] = jnp.zeros(acc_ref.shape, F32)

    def tile_start(tile):
        return tile * tk if isinstance(tile, int) else pl.multiple_of(tile * tk, tk)

    def scores(tile, slot):
        k = k_ref[pl.ds(tile_start(tile), tk), :]
        for mp in range(2):
            st = jnp.dot(k, qt_ref[mp], preferred_element_type=F32)
            s_ref[slot, mp] = st
            smax_ref[slot, mp] = jnp.max(st, axis=0, keepdims=True)

    def update(tile, slot):
        vt = vt_ref[:, pl.ds(tile_start(tile), tk)]
        for mp in range(2):
            m_old = m_ref[mp]
            m_new = jnp.maximum(m_old, smax_ref[slot, mp])
            alpha = jnp.exp2(m_old - m_new)
            p = jnp.exp2(s_ref[slot, mp] - m_new)
            l_ref[mp] = alpha * l_ref[mp] + jnp.sum(p, axis=0, keepdims=True)
            acc_ref[mp] = alpha * acc_ref[mp] + jnp.dot(vt, p.astype(BF16), preferred_element_type=F32)
            m_ref[mp] = m_new

    scores(0, 0)

    def body(j, carry):
        t = 2 * j
        scores(t + 1, 1)
        update(t, 0)
        scores(t + 2, 0)
        update(t + 1, 1)
        return carry

    lax.fori_loop(0, nk // 2 - 1, body, 0)
    scores(nk - 1, 1)
    update(nk - 2, 0)
    update(nk - 1, 1)

    lam_p = lam_ref[...]
    t1 = jnp.sum(lam_p[0:1, :] * lam_p[1:2, :], axis=-1, keepdims=True)
    t2 = jnp.sum(lam_p[2:3, :] * lam_p[3:4, :], axis=-1, keepdims=True)
    lam = jnp.exp(t1) - jnp.exp(t2) + lambda_init
    ot = acc_ref[0] / l_ref[0] - lam * (acc_ref[1] / l_ref[1])
    ms = jnp.mean(ot * ot, axis=0, keepdims=True)
    o = (ot * lax.rsqrt(ms + EPS)).T
    o_ref[...] = (o * sub_ref[...] * (1.0 - lambda_init)).astype(o_ref.dtype)


def _attn_alias_kernel(lam_ref, sub_ref, q_ref, k_ref, v_ref, prev_ref, o_ref, *scratch, **kw):
    del prev_ref
    _attn_kernel(lam_ref, sub_ref, q_ref, k_ref, v_ref, o_ref, *scratch, **kw)


def _attention_group(qkv, lam_p, sub_w, row0, nseq, s, lambda_init, prev_out):
    tq = _pick(s, 512)
    tk = _pick(s // 2, 1024)
    nq = s // tq
    assert row0 % s == 0 and (s // tk) % 2 == 0
    kw = dict(tk=tk, lambda_init=lambda_init)
    in_specs = [
        pl.BlockSpec((4, DA_HEAD_DIM), lambda b, h, i: (0, 0)),
        pl.BlockSpec((1, DA_V_DIM), lambda b, h, i: (0, 0)),
        pl.BlockSpec((tq, DA_V_DIM), lambda b, h, i: (row0 // tq + b * nq + i, h)),
        pl.BlockSpec((s, DA_V_DIM), lambda b, h, i: (row0 // s + b, DA_HEADS + h)),
        pl.BlockSpec((s, DA_V_DIM), lambda b, h, i: (row0 // s + b, 2 * DA_HEADS + h)),
    ]
    args = [lam_p, sub_w, qkv, qkv, qkv]
    if prev_out is None:
        kern, aliases = functools.partial(_attn_kernel, **kw), {}
    else:
        kern, aliases = functools.partial(_attn_alias_kernel, **kw), {len(args): 0}
        in_specs.append(pl.BlockSpec(memory_space=pl.ANY))
        args.append(prev_out)
    return pl.pallas_call(
        kern,
        grid=(nseq, DA_HEADS, nq),
        in_specs=in_specs,
        out_specs=pl.BlockSpec((tq, DA_V_DIM), lambda b, h, i: (row0 // tq + b * nq + i, h)),
        out_shape=jax.ShapeDtypeStruct((qkv.shape[0], DA_HEADS * DA_V_DIM), BF16),
        input_output_aliases=aliases,
        scratch_shapes=[
            pltpu.VMEM((DA_V_DIM, s), BF16),
            pltpu.VMEM((2, DA_V_DIM, tq), BF16),
            pltpu.VMEM((2, 2, tk, tq), F32),
            pltpu.VMEM((2, 2, 1, tq), F32),
            pltpu.VMEM((2, 1, tq), F32),
            pltpu.VMEM((2, 1, tq), F32),
            pltpu.VMEM((2, DA_V_DIM, tq), F32),
        ],
        compiler_params=_cparams(3),
        name="diff_attention",
    )(*args)


def _matmul_res_kernel(a_ref, w_ref, *refs, offs):
    r_refs, o_ref = refs[:-1], refs[-1]
    i = pl.program_id(0)
    y = jnp.dot(a_ref[...], w_ref[...], preferred_element_type=F32)
    for p, r_ref in enumerate(r_refs):
        @pl.when(_in_part(i, offs, p))
        def _(r_ref=r_ref):
            o_ref[...] = r_ref[...] + y


def _matmul_res(a, w, res_parts):
    t, k = a.shape
    n = w.shape[1]
    bm = _pick(math.gcd(*[p.shape[0] for p in res_parts]), 1024)
    bn = _pick(n, 512)
    offs = _part_offsets(res_parts, bm)
    col = lambda i, j, lo, cnt: jnp.where(jnp.logical_and(i >= lo, i < lo + cnt), j, 0)
    return pl.pallas_call(
        functools.partial(_matmul_res_kernel, offs=tuple(offs)),
        grid=(t // bm, n // bn),
        in_specs=[
            pl.BlockSpec((bm, k), lambda i, j: (i, 0)),
            pl.BlockSpec((k, bn), lambda i, j: (0, j)),
        ] + [_part_spec((bm, bn), offs, p, col) for p in range(len(res_parts))],
        out_specs=pl.BlockSpec((bm, bn), lambda i, j: (i, j)),
        out_shape=jax.ShapeDtypeStruct((t, n), F32),
        compiler_params=_cparams(2),
        name="matmul_residual",
    )(a, w, *res_parts)


def _ffn_kernel(x_ref, nw_ref, wg_ref, wu_ref, wd_ref, nf_ref, *refs, offs, final_norm):
    o_refs, (h_ref, acc_ref) = refs[:-2], refs[-2:]
    i = pl.program_id(0)
    j = pl.program_id(1)

    @pl.when(j == 0)
    def _():
        h_ref[...] = _rms(x_ref[...], nw_ref[...]).astype(BF16)
        acc_ref[...] = jnp.zeros_like(acc_ref)

    h = h_ref[...]
    g = jnp.dot(h, wg_ref[...], preferred_element_type=F32)
    u = jnp.dot(h, wu_ref[...], preferred_element_type=F32)
    a = (_silu(g) * u).astype(BF16)
    acc_ref[...] += jnp.dot(a, wd_ref[...], preferred_element_type=F32)

    last = j == pl.num_programs(1) - 1
    for p, o_ref in enumerate(o_refs):
        @pl.when(jnp.logical_and(last, _in_part(i, offs, p)))
        def _(o_ref=o_ref):
            r = acc_ref[...] + x_ref[...]
            if final_norm:
                r = _rms(r, nf_ref[...])
            o_ref[...] = r


def _ffn(x, nw, w_gu, w_down, nf, final_norm, out_rows):
    t, d = x.shape
    dff = w_down.shape[0]
    bm = _pick(math.gcd(*out_rows), 512)
    bf = _pick(dff, 512)
    nf_tiles = dff // bf
    outs = [jax.ShapeDtypeStruct((r, d), F32) for r in out_rows]
    offs = _part_offsets(outs, bm)
    return pl.pallas_call(
        functools.partial(_ffn_kernel, offs=tuple(offs), final_norm=final_norm),
        grid=(t // bm, nf_tiles),
        in_specs=[
            pl.BlockSpec((bm, d), lambda i, j: (i, 0)),
            pl.BlockSpec((1, d), lambda i, j: (0, 0)),
            pl.BlockSpec((d, bf), lambda i, j: (0, j)),
            pl.BlockSpec((d, bf), lambda i, j: (0, j + nf_tiles)),
            pl.BlockSpec((bf, d), lambda i, j: (j, 0)),
            pl.BlockSpec((1, d), lambda i, j: (0, 0)),
        ],
        out_specs=[_part_spec((bm, d), offs, p, lambda i, j, lo, cnt: 0) for p in range(len(outs))],
        out_shape=outs,
        scratch_shapes=[pltpu.VMEM((bm, d), BF16), pltpu.VMEM((bm, d), F32)],
        compiler_params=_cparams(2),
        name="swiglu_ffn",
    )(x, nw, w_gu, w_gu, w_down, nf)


def _ssd_in_kernel(x_ref, nw_ref, w_ref, wdt_ref, o_ref, dt_ref, h_ref):
    j = pl.program_id(1)

    @pl.when(j == 0)
    def _():
        h = _rms(x_ref[...], nw_ref[...]).astype(BF16)
        h_ref[...] = h
        dt_ref[...] = jnp.dot(h, wdt_ref[...], preferred_element_type=F32)

    o_ref[...] = jnp.dot(h_ref[...], w_ref[...], preferred_element_type=F32).astype(o_ref.dtype)


def _ssd_in_proj(x, nw, w_zx, w_dt):
    t, d = x.shape
    n = w_zx.shape[1]
    ndt = w_dt.shape[1]
    bm = _pick(t, 1024)
    bn = _pick(n, 512)
    return pl.pallas_call(
        _ssd_in_kernel,
        grid=(t // bm, n // bn),
        in_specs=[
            pl.BlockSpec((bm, d), lambda i, j: (i, 0)),
            pl.BlockSpec((1, d), lambda i, j: (0, 0)),
            pl.BlockSpec((d, bn), lambda i, j: (0, j)),
            pl.BlockSpec((d, ndt), lambda i, j: (0, 0)),
        ],
        out_specs=[
            pl.BlockSpec((bm, bn), lambda i, j: (i, j)),
            pl.BlockSpec((bm, ndt), lambda i, j: (i, 0)),
        ],
        out_shape=[jax.ShapeDtypeStruct((t, n), BF16), jax.ShapeDtypeStruct((t, ndt), F32)],
        scratch_shapes=[pltpu.VMEM((bm, d), BF16)],
        compiler_params=_cparams(2),
        name="ssd_in_proj",
    )(x, nw, w_zx, w_dt)


def _conv_kernel(flags_ref, prev_ref, cur_ref, next_ref, w_ref, b_ref, o_ref):
    f = flags_ref[pl.program_id(0)]
    at_start = (f & 1) == 1
    at_end = (f & 2) == 2
    x = cur_ref[...].astype(F32)
    bt = x.shape[0]
    row = lax.broadcasted_iota(jnp.int32, x.shape, 0)
    before = jnp.where(at_start, 0.0, prev_ref[SUBLANES - 1:SUBLANES, :].astype(F32))
    after = jnp.where(at_end, 0.0, next_ref[0:2, :].astype(F32))
    xm1 = jnp.where(row == 0, before, pltpu.roll(x, 1, 0))
    xp1 = jnp.where(row == bt - 1, after[0:1], pltpu.roll(x, bt - 1, 0))
    xp2 = jnp.where(row == bt - 2, after[0:1],
                    jnp.where(row == bt - 1, after[1:2], pltpu.roll(x, bt - 2, 0)))
    w = w_ref[...]
    y = xm1 * w[0:1] + x * w[1:2] + xp1 * w[2:3] + xp2 * w[3:4] + b_ref[...]
    o_ref[...] = _silu(y).astype(o_ref.dtype)


def _conv_silu(zx, conv_w, conv_b, seqs, col0):
    t = zx.shape[0]
    ncol = conv_w.shape[1]
    bt = _pick(math.gcd(*[s for _, s in seqs]), 512)
    bc = _pick(ncol, 1024)
    nblk = t // bt
    starts = {r for r, _ in seqs}
    ends = {r + s for r, s in seqs}
    flags = np.array([(1 if i * bt in starts else 0) | (2 if (i + 1) * bt in ends else 0)
                      for i in range(nblk)], np.int32)
    hb = bt // SUBLANES
    c0 = col0 // bc
    grid_spec = pltpu.PrefetchScalarGridSpec(
        num_scalar_prefetch=1,
        grid=(nblk, ncol // bc),
        in_specs=[
            pl.BlockSpec((SUBLANES, bc), lambda i, j, f: (jnp.maximum(i * hb - 1, 0), c0 + j)),
            pl.BlockSpec((bt, bc), lambda i, j, f: (i, c0 + j)),
            pl.BlockSpec((SUBLANES, bc), lambda i, j, f: (jnp.minimum((i + 1) * hb, t // SUBLANES - 1), c0 + j)),
            pl.BlockSpec((4, bc), lambda i, j, f: (0, j)),
            pl.BlockSpec((1, bc), lambda i, j, f: (0, j)),
        ],
        out_specs=pl.BlockSpec((bt, bc), lambda i, j, f: (i, j)),
    )
    return pl.pallas_call(
        _conv_kernel,
        grid_spec=grid_spec,
        out_shape=jax.ShapeDtypeStruct((t, ncol), BF16),
        compiler_params=_cparams(2),
        name="ssd_conv_silu",
    )(jnp.asarray(flags), zx, zx, zx, conv_w, conv_b)


def _pair_cols(mat, h0):
    rows = mat.shape[0]
    lane = lax.broadcasted_iota(jnp.int32, (rows, LANES), 1)
    a = jnp.broadcast_to(mat[:, h0:h0 + 1], (rows, LANES))
    b = jnp.broadcast_to(mat[:, h0 + 1:h0 + 2], (rows, LANES))
    return jnp.where(lane < SSD_HEAD_DIM, a, b)


def _group_cols(mat, h0):
    return jnp.concatenate([_pair_cols(mat, h0 + 2 * j) for j in range(SSD_HPG // 2)], axis=1)


def _scan_kernel(flags_ref, xs_ref, b_ref, c_ref, dt_ref, bias_ref, alog_ref, d_ref, *rest,
                 reverse):
    if reverse:
        yin_ref, y_ref, state_ref, ut_ref = rest
    else:
        y_ref, state_ref, ut_ref = rest
    q = SSD_CHUNK
    dir_off = SSD_HEADS if reverse else 0

    @pl.when(flags_ref[pl.program_id(0)] == 1)
    def _():
        state_ref[...] = jnp.zeros_like(state_ref)

    xdt_in = dt_ref[...] + bias_ref[...]
    dt = jnp.maximum(xdt_in, 0.0) + jnp.log(1.0 + jnp.exp(-jnp.abs(xdt_in)))
    da = dt * (-jnp.exp(alog_ref[...]))
    row = lax.broadcasted_iota(jnp.int32, (q, LANES), 0)
    cum = da
    k = 1
    while k < q:
        cum = cum + jnp.where(row >= k, pltpu.roll(cum, k, 0), 0.0)
        k *= 2
    tot = cum[q - 1:q, :]
    if reverse:
        u = da - cum
        scale_off = jnp.exp(tot + u)
        decay_end = jnp.exp(-u)
    else:
        u = cum
        scale_off = jnp.exp(u)
        decay_end = jnp.exp(tot - u)
    state_decay = jnp.exp(tot)
    ut_ref[...] = u.T

    li = lax.broadcasted_iota(jnp.int32, (q, q), 0)
    si = lax.broadcasted_iota(jnp.int32, (q, q), 1)
    allowed = (si >= li) if reverse else (li >= si)
    lane = lax.broadcasted_iota(jnp.int32, (q, LANES), 1)
    low = lane < SSD_HEAD_DIM

    for g in range(SSD_GROUPS):
        h0 = dir_off + g * SSD_HPG
        cols = slice(g * SSD_GROUP_W, (g + 1) * SSD_GROUP_W)
        xs = xs_ref[:, cols].astype(F32)
        bg = b_ref[:, g * SSD_D_STATE:(g + 1) * SSD_D_STATE]
        cg = c_ref[:, g * SSD_D_STATE:(g + 1) * SSD_D_STATE]
        cb = lax.dot_general(cg, bg, (((1,), (1,)), ((), ())), preferred_element_type=F32)
        xdt = xs * _group_cols(dt, h0)
        xdt_b = xdt.astype(BF16)
        xw = (xdt * _group_cols(decay_end, h0)).astype(BF16)
        st = state_ref[g]
        y = jnp.dot(cg, st.astype(BF16), preferred_element_type=F32) * _group_cols(scale_off, h0)
        state_ref[g] = (st * _group_cols(state_decay, h0)
                        + lax.dot_general(bg, xw, (((0,), (0,)), ((), ())), preferred_element_type=F32))
        tiles = []
        for j in range(SSD_HPG // 2):
            att = []
            for r in (2 * j, 2 * j + 1):
                h = h0 + r
                diff = u[:, h:h + 1] - ut_ref[h:h + 1, :]
                att.append((cb * jnp.exp(jnp.where(allowed, diff, -jnp.inf))).astype(BF16))
            xt = xdt_b[:, j * LANES:(j + 1) * LANES]
            zt = jnp.zeros_like(xt)
            rhs = jnp.concatenate([jnp.where(low, xt, zt), jnp.where(low, zt, xt)], axis=0)
            tiles.append(jnp.dot(jnp.concatenate(att, axis=1), rhs, preferred_element_type=F32))
        y = y + jnp.concatenate(tiles, axis=1)
        if reverse:
            y = y + yin_ref[:, cols].astype(F32)
        else:
            y = y + d_ref[:, cols] * xs
        y_ref[:, cols] = y.astype(y_ref.dtype)


def _ssd_scan(xbc, dt_raw, dt_bias, a_log, d_row, seqs, y_in):
    reverse = y_in is not None
    t = xbc.shape[0]
    q = SSD_CHUNK
    nchunk = t // q
    starts = {r // q for r, _ in seqs}
    ends = {(r + s) // q - 1 for r, s in seqs}
    order = list(range(nchunk))[::-1] if reverse else list(range(nchunk))
    flags = np.array([1 if c in (ends if reverse else starts) else 0 for c in order], np.int32)
    if reverse:
        blk = lambda c: nchunk - 1 - c
    else:
        blk = lambda c: c
    nxb = SSD_D_INNER // SSD_BC_W
    in_specs = [
        pl.BlockSpec((q, SSD_D_INNER), lambda c, f: (blk(c), 0)),
        pl.BlockSpec((q, SSD_BC_W), lambda c, f: (blk(c), nxb)),
        pl.BlockSpec((q, SSD_BC_W), lambda c, f: (blk(c), nxb + 1)),
        pl.BlockSpec((q, 2 * SSD_HEADS), lambda c, f: (blk(c), 0)),
        pl.BlockSpec((1, 2 * SSD_HEADS), lambda c, f: (0, 0)),
        pl.BlockSpec((1, 2 * SSD_HEADS), lambda c, f: (0, 0)),
        pl.BlockSpec((1, SSD_D_INNER), lambda c, f: (0, 0)),
    ]
    args = [jnp.asarray(flags), xbc, xbc, xbc, dt_raw, dt_bias, a_log, d_row]
    if reverse:
        in_specs.append(pl.BlockSpec((q, SSD_D_INNER), lambda c, f: (blk(c), 0)))
        args.append(y_in)
    grid_spec = pltpu.PrefetchScalarGridSpec(
        num_scalar_prefetch=1,
        grid=(nchunk,),
        in_specs=in_specs,
        out_specs=pl.BlockSpec((q, SSD_D_INNER), lambda c, f: (blk(c), 0)),
        scratch_shapes=[pltpu.VMEM((SSD_GROUPS, SSD_D_STATE, SSD_GROUP_W), F32),
                        pltpu.VMEM((2 * SSD_HEADS, q), F32)],
    )
    return pl.pallas_call(
        functools.partial(_scan_kernel, reverse=reverse),
        grid_spec=grid_spec,
        out_shape=jax.ShapeDtypeStruct((t, SSD_D_INNER), BF16),
        compiler_params=_cparams(1),
        name="ssd_scan_bwd" if reverse else "ssd_scan_fwd",
    )(*args)


def _ssd_out_kernel(y_ref, z_ref, nw_ref, w_ref, r_ref, o_ref, yn_ref):
    j = pl.program_id(1)

    @pl.when(j == 0)
    def _():
        for g in range(SSD_GROUPS):
            cols = slice(g * SSD_GROUP_W, (g + 1) * SSD_GROUP_W)
            gated = y_ref[:, cols].astype(F32) * _silu(z_ref[:, cols].astype(F32))
            yn_ref[:, cols] = _rms(gated, nw_ref[:, cols]).astype(BF16)

    o_ref[...] = r_ref[...] + jnp.dot(yn_ref[...], w_ref[...], preferred_element_type=F32)


def _ssd_out_proj(y, zx, nw, w, res):
    t, k = y.shape
    n = w.shape[1]
    bm = _pick(t, 512)
    bn = _pick(n, 512)
    return pl.pallas_call(
        _ssd_out_kernel,
        grid=(t // bm, n // bn),
        in_specs=[
            pl.BlockSpec((bm, k), lambda i, j: (i, 0)),
            pl.BlockSpec((bm, k), lambda i, j: (i, 0)),
            pl.BlockSpec((1, k), lambda i, j: (0, 0)),
            pl.BlockSpec((k, bn), lambda i, j: (0, j)),
            pl.BlockSpec((bm, bn), lambda i, j: (i, j)),
        ],
        out_specs=pl.BlockSpec((bm, bn), lambda i, j: (i, j)),
        out_shape=jax.ShapeDtypeStruct((t, n), F32),
        scratch_shapes=[pltpu.VMEM((bm, k), BF16)],
        compiler_params=_cparams(2),
        name="ssd_out_proj",
    )(y, zx, nw, w, res)


def _rope_tables(seqs, t):
    half = DA_HEAD_DIM // 2
    inv = 1.0 / (ROPE_THETA ** (jnp.arange(0, DA_HEAD_DIM, 2, dtype=F32) / DA_HEAD_DIM))
    pos = np.zeros((t,), np.float32)
    for r, s in seqs:
        pos[r:r + s] = np.arange(s, dtype=np.float32)
    ang = jnp.asarray(pos)[:, None] * inv[None, :]
    cos, sin = jnp.cos(ang), jnp.sin(ang)
    cos_t = jnp.concatenate([cos, cos, cos, cos], axis=1)
    sin_t = jnp.concatenate([-sin, -sin, sin, sin], axis=1)
    assert cos_t.shape == (t, LANES) and 4 * half == LANES
    return cos_t, sin_t


def _reorder_qk_columns(w_qkv):
    d = w_qkv.shape[0]
    half = DA_HEAD_DIM // 2
    qk = w_qkv[:, :2 * d].reshape(d, 2 * d // LANES, 2, 2, half)
    qk = qk.transpose(0, 1, 3, 2, 4).reshape(d, 2 * d)
    return jnp.concatenate([qk, w_qkv[:, 2 * d:]], axis=1)


def _seq_groups(seqs):
    groups = []
    for r, s in seqs:
        if groups and groups[-1][2] == s and groups[-1][0] + groups[-1][1] * s == r:
            groups[-1][1] += 1
        else:
            groups.append([r, 1, s])
    return [tuple(g) for g in groups]


def _attention_layer(x_parts, seqs, nw, w_qkv, w_o, lam_p, sub_w, lambda_init):
    t = sum(p.shape[0] for p in x_parts)
    cos_t, sin_t = _rope_tables(seqs, t)
    qkv = _qkv_proj(x_parts, nw, _reorder_qk_columns(w_qkv).astype(BF16), cos_t, sin_t)
    o = None
    for r, n, s in _seq_groups(seqs):
        o = _attention_group(qkv, lam_p, sub_w, r, n, s, lambda_init, o)
    return _matmul_res(o, w_o.astype(BF16), x_parts)


def _ssd_layer(x, seqs, nw, w_in, conv_w, conv_b, dt_bias, a_log, d_skip, norm_w, w_out):
    nzx = SSD_D_INNER + SSD_CONV_DIM
    zx, dt_raw = _ssd_in_proj(x, nw, w_in[:, :nzx].astype(BF16), w_in[:, nzx:].astype(BF16))
    xbc = _conv_silu(zx, conv_w, conv_b.reshape(1, -1), seqs, SSD_D_INNER)
    bias = dt_bias.reshape(1, -1)
    alog = a_log.reshape(1, -1)
    d_row = jnp.repeat(d_skip, SSD_HEAD_DIM).reshape(1, -1)
    y = _ssd_scan(xbc, dt_raw, bias, alog, d_row, seqs, None)
    y = _ssd_scan(xbc, dt_raw, bias, alog, d_row, seqs, y)
    return _ssd_out_proj(y, zx, norm_w.reshape(1, -1), w_out.astype(BF16), x)


def _trunk(x_parts, seqs, norm_mix, norm_ffn, norm_final, da_w_qkv, da_w_o, da_lambda_q1, da_lambda_k1,
           da_lambda_q2, da_lambda_k2, da_subln, ssd_w_in, ssd_conv_w, ssd_conv_b, ssd_dt_bias,
           ssd_a_log, ssd_d, ssd_norm, ssd_w_out, ffn_w_gu, ffn_w_down):
    depth = norm_mix.shape[0]
    nf = norm_final.reshape(1, -1)
    part_rows = [p.shape[0] for p in x_parts]
    x = x_parts
    for i in range(depth):
        j = i // 2
        nw = norm_mix[i].reshape(1, -1)
        if i % 2 == 0:
            lambda_init = 0.8 - 0.6 * math.exp(-0.3 * i)
            lam_p = jnp.stack([da_lambda_q1[j], da_lambda_k1[j], da_lambda_q2[j], da_lambda_k2[j]])
            y = _attention_layer(x, seqs, nw, da_w_qkv[j], da_w_o[j], lam_p,
                                 da_subln[j].reshape(1, -1), lambda_init)
        else:
            (x0,) = x
            y = _ssd_layer(x0, seqs, nw, ssd_w_in[j], ssd_conv_w[j], ssd_conv_b[j], ssd_dt_bias[j],
                           ssd_a_log[j], ssd_d[j], ssd_norm[j], ssd_w_out[j])
        last = i == depth - 1
        x = tuple(_ffn(y, norm_ffn[i].reshape(1, -1), ffn_w_gu[i].astype(BF16), ffn_w_down[i].astype(BF16),
                       nf, final_norm=last, out_rows=part_rows if last else [y.shape[0]]))
    return x


def kernel(x_prompt, x_sample, norm_mix, norm_ffn, norm_final, da_w_qkv, da_w_o, da_lambda_q1, da_lambda_k1, da_lambda_q2, da_lambda_k2, da_subln, ssd_w_in, ssd_conv_w, ssd_conv_b, ssd_dt_bias, ssd_a_log, ssd_d, ssd_norm, ssd_w_out, ffn_w_gu, ffn_w_down):
    d = x_prompt.shape[-1]
    seqs = []
    for arr in (x_prompt, x_sample):
        for _ in range(arr.shape[0]):
            seqs.append((sum(s for _, s in seqs), arr.shape[1]))
    x_parts = (x_prompt.reshape(-1, d), x_sample.reshape(-1, d))
    y_p, y_s = _trunk(x_parts, tuple(seqs), norm_mix, norm_ffn, norm_final, da_w_qkv, da_w_o,
                      da_lambda_q1, da_lambda_k1, da_lambda_q2, da_lambda_k2, da_subln, ssd_w_in,
                      ssd_conv_w, ssd_conv_b, ssd_dt_bias, ssd_a_log, ssd_d, ssd_norm, ssd_w_out,
                      ffn_w_gu, ffn_w_down)
    return y_p.reshape(x_prompt.shape), y_s.reshape(x_sample.shape)
```

```python
import functools
import math

import numpy as np
import jax
import jax.numpy as jnp
from jax import lax
from jax.experimental import pallas as pl
from jax.experimental.pallas import tpu as pltpu

F32 = jnp.float32
BF16 = jnp.bfloat16

EPS = 1e-5
LOG2E = 1.4426950408889634
LANES = 128
SUBLANES = 8
VMEM_LIMIT = 56 * 1024 * 1024

DA_HEADS = 16
DA_HEAD_DIM = 64
DA_V_DIM = 2 * DA_HEAD_DIM
ROPE_THETA = 10000.0

SSD_HEAD_DIM = 64
SSD_GROUPS = 8
SSD_HPG = 8
SSD_HEADS = SSD_GROUPS * SSD_HPG
SSD_D_STATE = 128
SSD_D_INNER = SSD_HEADS * SSD_HEAD_DIM
SSD_CHUNK = 128
SSD_GROUP_W = SSD_HPG * SSD_HEAD_DIM
SSD_BC_W = SSD_GROUPS * SSD_D_STATE
SSD_CONV_DIM = SSD_D_INNER + 2 * SSD_BC_W


def _cparams(n_axes):
    return pltpu.CompilerParams(dimension_semantics=("arbitrary",) * n_axes,
                                vmem_limit_bytes=VMEM_LIMIT)


def _rms(x, w):
    ms = jnp.mean(x * x, axis=-1, keepdims=True)
    return x * lax.rsqrt(ms + EPS) * w


def _silu(x):
    return x / (1.0 + jnp.exp(-x))


def _pick(n, pref):
    b = min(n, pref)
    while n % b:
        b //= 2
    return b


def _part_offsets(parts, bm):
    offs = [0]
    for p in parts:
        assert p.shape[0] % bm == 0
        offs.append(offs[-1] + p.shape[0] // bm)
    return offs


def _part_spec(block, offs, p, col_fn):
    lo, n = offs[p], offs[p + 1] - offs[p]
    return pl.BlockSpec(block, lambda i, j: (jnp.clip(i - lo, 0, n - 1), col_fn(i, j, lo, n)))


def _in_part(i, offs, p):
    return jnp.logical_and(i >= offs[p], i < offs[p + 1])


def _qkv_kernel(*refs, offs, n_q_tiles, n_rope_tiles):
    nparts = len(offs) - 1
    x_refs = refs[:nparts]
    nw_ref, w_ref, cos_ref, sin_ref, o_ref, h_ref = refs[nparts:]
    i = pl.program_id(0)
    j = pl.program_id(1)

    for p in range(nparts):
        @pl.when(jnp.logical_and(j == 0, _in_part(i, offs, p)))
        def _(p=p):
            h_ref[...] = _rms(x_refs[p][...], nw_ref[...]).astype(BF16)

    y = jnp.dot(h_ref[...], w_ref[...], preferred_element_type=F32)
    rope = j < n_rope_tiles
    c = jnp.where(rope, cos_ref[...], 1.0)
    s = jnp.where(rope, sin_ref[...], 0.0)
    scale = jnp.where(j < n_q_tiles, LOG2E * DA_HEAD_DIM ** -0.5, 1.0)
    c = c * scale
    s = s * scale
    for t in range(y.shape[1] // LANES):
        yt = y[:, t * LANES:(t + 1) * LANES]
        o_ref[:, t * LANES:(t + 1) * LANES] = (yt * c + pltpu.roll(yt, LANES // 2, 1) * s).astype(o_ref.dtype)


def _qkv_proj(x_parts, nw, w, cos_t, sin_t):
    d = x_parts[0].shape[1]
    t = sum(p.shape[0] for p in x_parts)
    n = w.shape[1]
    bm = _pick(math.gcd(*[p.shape[0] for p in x_parts]), 1024)
    bn = _pick(d, 512)
    offs = _part_offsets(x_parts, bm)
    return pl.pallas_call(
        functools.partial(_qkv_kernel, offs=tuple(offs), n_q_tiles=d // bn, n_rope_tiles=2 * d // bn),
        grid=(t // bm, n // bn),
        in_specs=[_part_spec((bm, d), offs, p, lambda i, j, lo, cnt: 0) for p in range(len(x_parts))] + [
            pl.BlockSpec((1, d), lambda i, j: (0, 0)),
            pl.BlockSpec((d, bn), lambda i, j: (0, j)),
            pl.BlockSpec((bm, LANES), lambda i, j: (i, 0)),
            pl.BlockSpec((bm, LANES), lambda i, j: (i, 0)),
        ],
        out_specs=pl.BlockSpec((bm, bn), lambda i, j: (i, j)),
        out_shape=jax.ShapeDtypeStruct((t, n), BF16),
        scratch_shapes=[pltpu.VMEM((bm, d), BF16)],
        compiler_params=_cparams(2),
        name="qkv_proj",
    )(*x_parts, nw, w, cos_t, sin_t)


def _attn_kernel(lam_ref, sub_ref, q_ref, k_ref, v_ref, o_ref, vt_ref, qt_ref, s_ref, smax_ref,
                 m_ref, acc_ref, *, tk, lambda_init):
    tq = q_ref.shape[0]
    nk = k_ref.shape[0] // tk

    @pl.when(pl.program_id(2) == 0)
    def _():
        def transpose_chunk(c, carry):
            start = pl.multiple_of(c * tk, tk)
            vt_ref[0:DA_V_DIM, pl.ds(start, tk)] = v_ref[pl.ds(start, tk), :].astype(F32).T.astype(BF16)
            return carry
        lax.fori_loop(0, nk, transpose_chunk, 0)
        extra = lax.broadcasted_iota(jnp.int32, (vt_ref.shape[0] - DA_V_DIM, vt_ref.shape[1]), 0)
        vt_ref[DA_V_DIM:, :] = jnp.where(extra == 0, 1.0, 0.0).astype(BF16)

    qt = q_ref[...].astype(F32).T
    row = lax.broadcasted_iota(jnp.int32, qt.shape, 0)
    in_map0 = (row % DA_HEAD_DIM) < (DA_HEAD_DIM // 2)
    qt_ref[0] = jnp.where(in_map0, qt, 0.0).astype(BF16)
    qt_ref[1] = jnp.where(in_map0, 0.0, qt).astype(BF16)
    m_ref[...] = jnp.full(m_ref.shape, -jnp.inf, F32)
    acc_ref[...] = jnp.zeros(acc_ref.shape, F32)

    def tile_start(tile):
        return tile * tk if isinstance(tile, int) else pl.multiple_of(tile * tk, tk)

    def scores(tile, slot):
        k = k_ref[pl.ds(tile_start(tile), tk), :]
        for mp in range(2):
            st = jnp.dot(k, qt_ref[mp], preferred_element_type=F32)
            s_ref[slot, mp] = st
            smax_ref[slot, mp] = jnp.max(st, axis=0, keepdims=True)

    def update(tile, slot):
        vt = vt_ref[:, pl.ds(tile_start(tile), tk)]
        for mp in range(2):
            m_old = m_ref[mp]
            m_new = jnp.maximum(m_old, smax_ref[slot, mp])
            alpha = jnp.exp2(m_old - m_new)
            p = jnp.exp2(s_ref[slot, mp] - m_new)
            acc_ref[mp] = alpha * acc_ref[mp] + jnp.dot(vt, p.astype(BF16), preferred_element_type=F32)
            m_ref[mp] = m_new

    scores(0, 0)

    def stage_pair(j):
        t = 2 * j
        scores(t + 1, 1)
        update(t, 0)
        scores(t + 2, 0)
        update(t + 1, 1)

    def body(i, carry):
        stage_pair(2 * i)
        stage_pair(2 * i + 1)
        return carry

    n_pairs = nk // 2 - 1
    lax.fori_loop(0, n_pairs // 2, body, 0)
    if n_pairs % 2:
        stage_pair(n_pairs - 1)
    scores(nk - 1, 1)
    update(nk - 2, 0)
    update(nk - 1, 1)

    lam_p = lam_ref[...]
    t1 = jnp.sum(lam_p[0:1, :] * lam_p[1:2, :], axis=-1, keepdims=True)
    t2 = jnp.sum(lam_p[2:3, :] * lam_p[3:4, :], axis=-1, keepdims=True)
    lam = jnp.exp(t1) - jnp.exp(t2) + lambda_init
    acc0, acc1 = acc_ref[0], acc_ref[1]
    ot = (acc0[:DA_V_DIM] / acc0[DA_V_DIM:DA_V_DIM + 1]
          - lam * (acc1[:DA_V_DIM] / acc1[DA_V_DIM:DA_V_DIM + 1]))
    ms = jnp.mean(ot * ot, axis=0, keepdims=True)
    o = (ot * lax.rsqrt(ms + EPS)).T
    o_ref[...] = (o * sub_ref[...] * (1.0 - lambda_init)).astype(o_ref.dtype)


def _attn_alias_kernel(lam_ref, sub_ref, q_ref, k_ref, v_ref, prev_ref, o_ref, *scratch, **kw):
    del prev_ref
    _attn_kernel(lam_ref, sub_ref, q_ref, k_ref, v_ref, o_ref, *scratch, **kw)


def _attention_group(qkv, lam_p, sub_w, row0, nseq, s, lambda_init, prev_out):
    tq = _pick(s, 512)
    tk = _pick(s // 2, 1024)
    nq = s // tq
    assert row0 % s == 0 and (s // tk) % 2 == 0
    kw = dict(tk=tk, lambda_init=lambda_init)
    in_specs = [
        pl.BlockSpec((4, DA_HEAD_DIM), lambda b, h, i: (0, 0)),
        pl.BlockSpec((1, DA_V_DIM), lambda b, h, i: (0, 0)),
        pl.BlockSpec((tq, DA_V_DIM), lambda b, h, i: (row0 // tq + b * nq + i, h)),
        pl.BlockSpec((s, DA_V_DIM), lambda b, h, i: (row0 // s + b, DA_HEADS + h)),
        pl.BlockSpec((s, DA_V_DIM), lambda b, h, i: (row0 // s + b, 2 * DA_HEADS + h)),
    ]
    args = [lam_p, sub_w, qkv, qkv, qkv]
    if prev_out is None:
        kern, aliases = functools.partial(_attn_kernel, **kw), {}
    else:
        kern, aliases = functools.partial(_attn_alias_kernel, **kw), {len(args): 0}
        in_specs.append(pl.BlockSpec(memory_space=pl.ANY))
        args.append(prev_out)
    return pl.pallas_call(
        kern,
        grid=(nseq, DA_HEADS, nq),
        in_specs=in_specs,
        out_specs=pl.BlockSpec((tq, DA_V_DIM), lambda b, h, i: (row0 // tq + b * nq + i, h)),
        out_shape=jax.ShapeDtypeStruct((qkv.shape[0], DA_HEADS * DA_V_DIM), BF16),
        input_output_aliases=aliases,
        scratch_shapes=[
            pltpu.VMEM((DA_V_DIM + 16, s), BF16),
            pltpu.VMEM((2, DA_V_DIM, tq), BF16),
            pltpu.VMEM((2, 2, tk, tq), F32),
            pltpu.VMEM((2, 2, 1, tq), F32),
            pltpu.VMEM((2, 1, tq), F32),
            pltpu.VMEM((2, DA_V_DIM + 16, tq), F32),
        ],
        compiler_params=_cparams(3),
        name="diff_attention",
    )(*args)


def _matmul_res_kernel(a_ref, w_ref, *refs, offs):
    r_refs, o_ref = refs[:-1], refs[-1]
    i = pl.program_id(0)
    y = jnp.dot(a_ref[...], w_ref[...], preferred_element_type=F32)
    for p, r_ref in enumerate(r_refs):
        @pl.when(_in_part(i, offs, p))
        def _(r_ref=r_ref):
            o_ref[...] = r_ref[...] + y


def _matmul_res(a, w, res_parts):
    t, k = a.shape
    n = w.shape[1]
    bm = _pick(math.gcd(*[p.shape[0] for p in res_parts]), 1024)
    bn = _pick(n, 512)
    offs = _part_offsets(res_parts, bm)
    col = lambda i, j, lo, cnt: jnp.where(jnp.logical_and(i >= lo, i < lo + cnt), j, 0)
    return pl.pallas_call(
        functools.partial(_matmul_res_kernel, offs=tuple(offs)),
        grid=(t // bm, n // bn),
        in_specs=[
            pl.BlockSpec((bm, k), lambda i, j: (i, 0)),
            pl.BlockSpec((k, bn), lambda i, j: (0, j)),
        ] + [_part_spec((bm, bn), offs, p, col) for p in range(len(res_parts))],
        out_specs=pl.BlockSpec((bm, bn), lambda i, j: (i, j)),
        out_shape=jax.ShapeDtypeStruct((t, n), F32),
        compiler_params=_cparams(2),
        name="matmul_residual",
    )(a, w, *res_parts)


def _ffn_kernel(x_ref, nw_ref, wg_ref, wu_ref, wd_ref, nf_ref, *refs, offs, final_norm):
    o_refs, (h_ref, acc_ref) = refs[:-2], refs[-2:]
    i = pl.program_id(0)
    j = pl.program_id(1)

    @pl.when(j == 0)
    def _():
        h_ref[...] = _rms(x_ref[...], nw_ref[...]).astype(BF16)
        acc_ref[...] = jnp.zeros_like(acc_ref)

    h = h_ref[...]
    g = jnp.dot(h, wg_ref[...], preferred_element_type=F32)
    u = jnp.dot(h, wu_ref[...], preferred_element_type=F32)
    a = (_silu(g) * u).astype(BF16)
    acc_ref[...] += jnp.dot(a, wd_ref[...], preferred_element_type=F32)

    last = j == pl.num_programs(1) - 1
    for p, o_ref in enumerate(o_refs):
        @pl.when(jnp.logical_and(last, _in_part(i, offs, p)))
        def _(o_ref=o_ref):
            r = acc_ref[...] + x_ref[...]
            if final_norm:
                r = _rms(r, nf_ref[...])
            o_ref[...] = r


def _ffn(x, nw, w_gu, w_down, nf, final_norm, out_rows):
    t, d = x.shape
    dff = w_down.shape[0]
    bm = _pick(math.gcd(*out_rows), 512)
    bf = _pick(dff, 512)
    nf_tiles = dff // bf
    outs = [jax.ShapeDtypeStruct((r, d), F32) for r in out_rows]
    offs = _part_offsets(outs, bm)
    return pl.pallas_call(
        functools.partial(_ffn_kernel, offs=tuple(offs), final_norm=final_norm),
        grid=(t // bm, nf_tiles),
        in_specs=[
            pl.BlockSpec((bm, d), lambda i, j: (i, 0)),
            pl.BlockSpec((1, d), lambda i, j: (0, 0)),
            pl.BlockSpec((d, bf), lambda i, j: (0, j)),
            pl.BlockSpec((d, bf), lambda i, j: (0, j + nf_tiles)),
            pl.BlockSpec((bf, d), lambda i, j: (j, 0)),
            pl.BlockSpec((1, d), lambda i, j: (0, 0)),
        ],
        out_specs=[_part_spec((bm, d), offs, p, lambda i, j, lo, cnt: 0) for p in range(len(outs))],
        out_shape=outs,
        scratch_shapes=[pltpu.VMEM((bm, d), BF16), pltpu.VMEM((bm, d), F32)],
        compiler_params=_cparams(2),
        name="swiglu_ffn",
    )(x, nw, w_gu, w_gu, w_down, nf)


def _ssd_in_kernel(x_ref, nw_ref, w_ref, wdt_ref, o_ref, dt_ref, h_ref):
    j = pl.program_id(1)

    @pl.when(j == 0)
    def _():
        h = _rms(x_ref[...], nw_ref[...]).astype(BF16)
        h_ref[...] = h
        dt_ref[...] = jnp.dot(h, wdt_ref[...], preferred_element_type=F32)

    o_ref[...] = jnp.dot(h_ref[...], w_ref[...], preferred_element_type=F32).astype(o_ref.dtype)


def _ssd_in_proj(x, nw, w_zx, w_dt):
    t, d = x.shape
    n = w_zx.shape[1]
    ndt = w_dt.shape[1]
    bm = _pick(t, 1024)
    bn = _pick(n, 512)
    return pl.pallas_call(
        _ssd_in_kernel,
        grid=(t // bm, n // bn),
        in_specs=[
            pl.BlockSpec((bm, d), lambda i, j: (i, 0)),
            pl.BlockSpec((1, d), lambda i, j: (0, 0)),
            pl.BlockSpec((d, bn), lambda i, j: (0, j)),
            pl.BlockSpec((d, ndt), lambda i, j: (0, 0)),
        ],
        out_specs=[
            pl.BlockSpec((bm, bn), lambda i, j: (i, j)),
            pl.BlockSpec((bm, ndt), lambda i, j: (i, 0)),
        ],
        out_shape=[jax.ShapeDtypeStruct((t, n), BF16), jax.ShapeDtypeStruct((t, ndt), F32)],
        scratch_shapes=[pltpu.VMEM((bm, d), BF16)],
        compiler_params=_cparams(2),
        name="ssd_in_proj",
    )(x, nw, w_zx, w_dt)


def _conv_kernel(flags_ref, prev_ref, cur_ref, next_ref, w_ref, b_ref, o_ref):
    f = flags_ref[pl.program_id(0)]
    at_start = (f & 1) == 1
    at_end = (f & 2) == 2
    x = cur_ref[...].astype(F32)
    bt = x.shape[0]
    row = lax.broadcasted_iota(jnp.int32, x.shape, 0)
    before = jnp.where(at_start, 0.0, prev_ref[SUBLANES - 1:SUBLANES, :].astype(F32))
    after = jnp.where(at_end, 0.0, next_ref[0:2, :].astype(F32))
    xm1 = jnp.where(row == 0, before, pltpu.roll(x, 1, 0))
    xp1 = jnp.where(row == bt - 1, after[0:1], pltpu.roll(x, bt - 1, 0))
    xp2 = jnp.where(row == bt - 2, after[0:1],
                    jnp.where(row == bt - 1, after[1:2], pltpu.roll(x, bt - 2, 0)))
    w = w_ref[...]
    y = xm1 * w[0:1] + x * w[1:2] + xp1 * w[2:3] + xp2 * w[3:4] + b_ref[...]
    o_ref[...] = _silu(y).astype(o_ref.dtype)


def _conv_silu(zx, conv_w, conv_b, seqs, col0):
    t = zx.shape[0]
    ncol = conv_w.shape[1]
    bt = _pick(math.gcd(*[s for _, s in seqs]), 512)
    bc = _pick(ncol, 1024)
    nblk = t // bt
    starts = {r for r, _ in seqs}
    ends = {r + s for r, s in seqs}
    flags = np.array([(1 if i * bt in starts else 0) | (2 if (i + 1) * bt in ends else 0)
                      for i in range(nblk)], np.int32)
    hb = bt // SUBLANES
    c0 = col0 // bc
    grid_spec = pltpu.PrefetchScalarGridSpec(
        num_scalar_prefetch=1,
        grid=(nblk, ncol // bc),
        in_specs=[
            pl.BlockSpec((SUBLANES, bc), lambda i, j, f: (jnp.maximum(i * hb - 1, 0), c0 + j)),
            pl.BlockSpec((bt, bc), lambda i, j, f: (i, c0 + j)),
            pl.BlockSpec((SUBLANES, bc), lambda i, j, f: (jnp.minimum((i + 1) * hb, t // SUBLANES - 1), c0 + j)),
            pl.BlockSpec((4, bc), lambda i, j, f: (0, j)),
            pl.BlockSpec((1, bc), lambda i, j, f: (0, j)),
        ],
        out_specs=pl.BlockSpec((bt, bc), lambda i, j, f: (i, j)),
    )
    return pl.pallas_call(
        _conv_kernel,
        grid_spec=grid_spec,
        out_shape=jax.ShapeDtypeStruct((t, ncol), BF16),
        compiler_params=_cparams(2),
        name="ssd_conv_silu",
    )(jnp.asarray(flags), zx, zx, zx, conv_w, conv_b)


def _scan_kernel(flags_ref, xs_ref, b_ref, c_ref, dt_ref, bias_ref, alog_ref, d_ref, *rest,
                 reverse):
    if reverse:
        yin_ref, y_ref, state_ref, ut_ref, wt_ref = rest
    else:
        y_ref, state_ref, ut_ref, wt_ref = rest
    q = SSD_CHUNK
    dir_off = SSD_HEADS if reverse else 0

    @pl.when(flags_ref[pl.program_id(0)] == 1)
    def _():
        state_ref[...] = jnp.zeros_like(state_ref)

    xdt_in = dt_ref[...] + bias_ref[...]
    dt = jnp.maximum(xdt_in, 0.0) + jnp.log(1.0 + jnp.exp(-jnp.abs(xdt_in)))
    da = dt * (-jnp.exp(alog_ref[...]))
    row = lax.broadcasted_iota(jnp.int32, (q, LANES), 0)
    cum = da
    k = 1
    while k < q:
        cum = cum + jnp.where(row >= k, pltpu.roll(cum, k, 0), 0.0)
        k *= 2
    tot = cum[q - 1:q, :]
    u = (tot + da - cum) if reverse else cum
    state_decay = jnp.exp(tot)
    u2 = u * LOG2E
    ut_ref[...] = (u2 - jnp.log2(dt)).T
    wt_ref[...] = (dt * jnp.exp(tot - u)).T

    li = lax.broadcasted_iota(jnp.int32, (q, q), 0)
    si = lax.broadcasted_iota(jnp.int32, (q, q), 1)
    allowed = (si >= li) if reverse else (li >= si)
    low = lax.broadcasted_iota(jnp.int32, (q, LANES), 1) < SSD_HEAD_DIM

    for g in range(SSD_GROUPS):
        bg = b_ref[:, g * SSD_D_STATE:(g + 1) * SSD_D_STATE]
        cg = c_ref[:, g * SSD_D_STATE:(g + 1) * SSD_D_STATE]
        bt = bg.astype(F32).T
        cb = jnp.dot(cg, bt.astype(BF16), preferred_element_type=F32)
        st = state_ref[g]
        y_off = jnp.dot(cg, st.astype(BF16), preferred_element_type=F32)
        for j in range(SSD_HPG // 2):
            cols = slice(g * SSD_GROUP_W + j * LANES, g * SSD_GROUP_W + (j + 1) * LANES)
            tile = slice(j * LANES, (j + 1) * LANES)
            att, inj, scale = [], [], []
            for r in (2 * j, 2 * j + 1):
                h = dir_off + g * SSD_HPG + r
                ucol = jnp.broadcast_to(u2[:, h:h + 1], (q, q))
                decay_dt = jnp.exp2(jnp.where(allowed, ucol - ut_ref[h:h + 1, :], -jnp.inf))
                att.append((cb * decay_dt).astype(BF16))
                inj.append((bt * wt_ref[h:h + 1, :]).astype(BF16))
                scale.append(jnp.exp2(ucol))
            xt = xs_ref[:, cols]
            zt = jnp.zeros_like(xt)
            rhs = jnp.concatenate([jnp.where(low, xt, zt), jnp.where(low, zt, xt)], axis=0)
            lhs = jnp.concatenate([jnp.concatenate(att, axis=1), jnp.concatenate(inj, axis=1)], axis=0)
            res = jnp.dot(lhs, rhs, preferred_element_type=F32)
            y = res[:q] + y_off[:, tile] * jnp.where(low, scale[0], scale[1])
            if reverse:
                y = y + yin_ref[:, cols].astype(F32)
            else:
                y = y + d_ref[:, cols] * xt.astype(F32)
            y_ref[:, cols] = y.astype(y_ref.dtype)
            h = dir_off + g * SSD_HPG + 2 * j
            keep = jnp.where(low[0:1], state_decay[:, h:h + 1], state_decay[:, h + 1:h + 2])
            state_ref[g, :, tile] = st[:, tile] * keep + res[q:]


def _ssd_scan(xbc, dt_raw, dt_bias, a_log, d_row, seqs, y_in):
    reverse = y_in is not None
    t = xbc.shape[0]
    q = SSD_CHUNK
    nchunk = t // q
    starts = {r // q for r, _ in seqs}
    ends = {(r + s) // q - 1 for r, s in seqs}
    order = list(range(nchunk))[::-1] if reverse else list(range(nchunk))
    flags = np.array([1 if c in (ends if reverse else starts) else 0 for c in order], np.int32)
    if reverse:
        blk = lambda c: nchunk - 1 - c
    else:
        blk = lambda c: c
    nxb = SSD_D_INNER // SSD_BC_W
    in_specs = [
        pl.BlockSpec((q, SSD_D_INNER), lambda c, f: (blk(c), 0)),
        pl.BlockSpec((q, SSD_BC_W), lambda c, f: (blk(c), nxb)),
        pl.BlockSpec((q, SSD_BC_W), lambda c, f: (blk(c), nxb + 1)),
        pl.BlockSpec((q, 2 * SSD_HEADS), lambda c, f: (blk(c), 0)),
        pl.BlockSpec((1, 2 * SSD_HEADS), lambda c, f: (0, 0)),
        pl.BlockSpec((1, 2 * SSD_HEADS), lambda c, f: (0, 0)),
        pl.BlockSpec((1, SSD_D_INNER), lambda c, f: (0, 0)),
    ]
    args = [jnp.asarray(flags), xbc, xbc, xbc, dt_raw, dt_bias, a_log, d_row]
    if reverse:
        in_specs.append(pl.BlockSpec((q, SSD_D_INNER), lambda c, f: (blk(c), 0)))
        args.append(y_in)
    grid_spec = pltpu.PrefetchScalarGridSpec(
        num_scalar_prefetch=1,
        grid=(nchunk,),
        in_specs=in_specs,
        out_specs=pl.BlockSpec((q, SSD_D_INNER), lambda c, f: (blk(c), 0)),
        scratch_shapes=[pltpu.VMEM((SSD_GROUPS, SSD_D_STATE, SSD_GROUP_W), F32),
                        pltpu.VMEM((2 * SSD_HEADS, q), F32),
                        pltpu.VMEM((2 * SSD_HEADS, q), F32)],
    )
    return pl.pallas_call(
        functools.partial(_scan_kernel, reverse=reverse),
        grid_spec=grid_spec,
        out_shape=jax.ShapeDtypeStruct((t, SSD_D_INNER), BF16),
        compiler_params=_cparams(1),
        name="ssd_scan_bwd" if reverse else "ssd_scan_fwd",
    )(*args)


def _ssd_out_kernel(y_ref, z_ref, nw_ref, w_ref, r_ref, o_ref, yn_ref):
    j = pl.program_id(1)

    @pl.when(j == 0)
    def _():
        for g in range(SSD_GROUPS):
            cols = slice(g * SSD_GROUP_W, (g + 1) * SSD_GROUP_W)
            gated = y_ref[:, cols].astype(F32) * _silu(z_ref[:, cols].astype(F32))
            yn_ref[:, cols] = _rms(gated, nw_ref[:, cols]).astype(BF16)

    o_ref[...] = r_ref[...] + jnp.dot(yn_ref[...], w_ref[...], preferred_element_type=F32)


def _ssd_out_proj(y, zx, nw, w, res):
    t, k = y.shape
    n = w.shape[1]
    bm = _pick(t, 512)
    bn = _pick(n, 512)
    return pl.pallas_call(
        _ssd_out_kernel,
        grid=(t // bm, n // bn),
        in_specs=[
            pl.BlockSpec((bm, k), lambda i, j: (i, 0)),
            pl.BlockSpec((bm, k), lambda i, j: (i, 0)),
            pl.BlockSpec((1, k), lambda i, j: (0, 0)),
            pl.BlockSpec((k, bn), lambda i, j: (0, j)),
            pl.BlockSpec((bm, bn), lambda i, j: (i, j)),
        ],
        out_specs=pl.BlockSpec((bm, bn), lambda i, j: (i, j)),
        out_shape=jax.ShapeDtypeStruct((t, n), F32),
        scratch_shapes=[pltpu.VMEM((bm, k), BF16)],
        compiler_params=_cparams(2),
        name="ssd_out_proj",
    )(y, zx, nw, w, res)


def _rope_tables(seqs, t):
    half = DA_HEAD_DIM // 2
    inv = 1.0 / (ROPE_THETA ** (jnp.arange(0, DA_HEAD_DIM, 2, dtype=F32) / DA_HEAD_DIM))
    pos = np.zeros((t,), np.float32)
    for r, s in seqs:
        pos[r:r + s] = np.arange(s, dtype=np.float32)
    ang = jnp.asarray(pos)[:, None] * inv[None, :]
    cos, sin = jnp.cos(ang), jnp.sin(ang)
    cos_t = jnp.concatenate([cos, cos, cos, cos], axis=1)
    sin_t = jnp.concatenate([-sin, -sin, sin, sin], axis=1)
    assert cos_t.shape == (t, LANES) and 4 * half == LANES
    return cos_t, sin_t


def _reorder_qk_columns(w_qkv):
    d = w_qkv.shape[0]
    half = DA_HEAD_DIM // 2
    qk = w_qkv[:, :2 * d].reshape(d, 2 * d // LANES, 2, 2, half)
    qk = qk.transpose(0, 1, 3, 2, 4).reshape(d, 2 * d)
    return jnp.concatenate([qk, w_qkv[:, 2 * d:]], axis=1)


def _seq_groups(seqs):
    groups = []
    for r, s in seqs:
        if groups and groups[-1][2] == s and groups[-1][0] + groups[-1][1] * s == r:
            groups[-1][1] += 1
        else:
            groups.append([r, 1, s])
    return [tuple(g) for g in groups]


def _attention_layer(x_parts, seqs, nw, w_qkv, w_o, lam_p, sub_w, lambda_init):
    t = sum(p.shape[0] for p in x_parts)
    cos_t, sin_t = _rope_tables(seqs, t)
    qkv = _qkv_proj(x_parts, nw, _reorder_qk_columns(w_qkv).astype(BF16), cos_t, sin_t)
    o = None
    for r, n, s in _seq_groups(seqs):
        o = _attention_group(qkv, lam_p, sub_w, r, n, s, lambda_init, o)
    return _matmul_res(o, w_o.astype(BF16), x_parts)


def _ssd_layer(x, seqs, nw, w_in, conv_w, conv_b, dt_bias, a_log, d_skip, norm_w, w_out):
    nzx = SSD_D_INNER + SSD_CONV_DIM
    zx, dt_raw = _ssd_in_proj(x, nw, w_in[:, :nzx].astype(BF16), w_in[:, nzx:].astype(BF16))
    xbc = _conv_silu(zx, conv_w, conv_b.reshape(1, -1), seqs, SSD_D_INNER)
    bias = dt_bias.reshape(1, -1)
    alog = a_log.reshape(1, -1)
    d_row = jnp.repeat(d_skip, SSD_HEAD_DIM).reshape(1, -1)
    y = _ssd_scan(xbc, dt_raw, bias, alog, d_row, seqs, None)
    y = _ssd_scan(xbc, dt_raw, bias, alog, d_row, seqs, y)
    return _ssd_out_proj(y, zx, norm_w.reshape(1, -1), w_out.astype(BF16), x)


def _trunk(x_parts, seqs, norm_mix, norm_ffn, norm_final, da_w_qkv, da_w_o, da_lambda_q1, da_lambda_k1,
           da_lambda_q2, da_lambda_k2, da_subln, ssd_w_in, ssd_conv_w, ssd_conv_b, ssd_dt_bias,
           ssd_a_log, ssd_d, ssd_norm, ssd_w_out, ffn_w_gu, ffn_w_down):
    depth = norm_mix.shape[0]
    nf = norm_final.reshape(1, -1)
    part_rows = [p.shape[0] for p in x_parts]
    x = x_parts
    for i in range(depth):
        j = i // 2
        nw = norm_mix[i].reshape(1, -1)
        if i % 2 == 0:
            lambda_init = 0.8 - 0.6 * math.exp(-0.3 * i)
            lam_p = jnp.stack([da_lambda_q1[j], da_lambda_k1[j], da_lambda_q2[j], da_lambda_k2[j]])
            y = _attention_layer(x, seqs, nw, da_w_qkv[j], da_w_o[j], lam_p,
                                 da_subln[j].reshape(1, -1), lambda_init)
        else:
            (x0,) = x
            y = _ssd_layer(x0, seqs, nw, ssd_w_in[j], ssd_conv_w[j], ssd_conv_b[j], ssd_dt_bias[j],
                           ssd_a_log[j], ssd_d[j], ssd_norm[j], ssd_w_out[j])
        last = i == depth - 1
        x = tuple(_ffn(y, norm_ffn[i].reshape(1, -1), ffn_w_gu[i].astype(BF16), ffn_w_down[i].astype(BF16),
                       nf, final_norm=last, out_rows=part_rows if last else [y.shape[0]]))
    return x


def kernel(x_prompt, x_sample, norm_mix, norm_ffn, norm_final, da_w_qkv, da_w_o, da_lambda_q1, da_lambda_k1, da_lambda_q2, da_lambda_k2, da_subln, ssd_w_in, ssd_conv_w, ssd_conv_b, ssd_dt_bias, ssd_a_log, ssd_d, ssd_norm, ssd_w_out, ffn_w_gu, ffn_w_down):
    d = x_prompt.shape[-1]
    seqs = []
    for arr in (x_prompt, x_sample):
        for _ in range(arr.shape[0]):
            seqs.append((sum(s for _, s in seqs), arr.shape[1]))
    x_parts = (x_prompt.reshape(-1, d), x_sample.reshape(-1, d))
    y_p, y_s = _trunk(x_parts, tuple(seqs), norm_mix, norm_ffn, norm_final, da_w_qkv, da_w_o,
                      da_lambda_q1, da_lambda_k1, da_lambda_q2, da_lambda_k2, da_subln, ssd_w_in,
                      ssd_conv_w, ssd_conv_b, ssd_dt_bias, ssd_a_log, ssd_d, ssd_norm, ssd_w_out,
                      ffn_w_gu, ffn_w_down)
    return y_p.reshape(x_prompt.shape), y_s.reshape(x_sample.shape)
```

```python
import functools
import math

import numpy as np
import jax
import jax.numpy as jnp
from jax import lax
from jax.experimental import pallas as pl
from jax.experimental.pallas import tpu as pltpu

F32 = jnp.float32
BF16 = jnp.bfloat16

EPS = 1e-5
LOG2E = 1.4426950408889634
LANES = 128
SUBLANES = 8
VMEM_LIMIT = 56 * 1024 * 1024

DA_HEADS = 16
DA_HEAD_DIM = 64
DA_V_DIM = 2 * DA_HEAD_DIM
ROPE_THETA = 10000.0

SSD_HEAD_DIM = 64
SSD_GROUPS = 8
SSD_HPG = 8
SSD_HEADS = SSD_GROUPS * SSD_HPG
SSD_D_STATE = 128
SSD_D_INNER = SSD_HEADS * SSD_HEAD_DIM
SSD_CHUNK = 128
SSD_GROUP_W = SSD_HPG * SSD_HEAD_DIM
SSD_BC_W = SSD_GROUPS * SSD_D_STATE
SSD_CONV_DIM = SSD_D_INNER + 2 * SSD_BC_W


def _cparams(n_axes):
    return pltpu.CompilerParams(dimension_semantics=("arbitrary",) * n_axes,
                                vmem_limit_bytes=VMEM_LIMIT)


def _rms(x, w):
    ms = jnp.mean(x * x, axis=-1, keepdims=True)
    return x * lax.rsqrt(ms + EPS) * w


def _silu(x):
    return x / (1.0 + jnp.exp(-x))


def _pick(n, pref):
    b = min(n, pref)
    while n % b:
        b //= 2
    return b


def _part_offsets(parts, bm):
    offs = [0]
    for p in parts:
        assert p.shape[0] % bm == 0
        offs.append(offs[-1] + p.shape[0] // bm)
    return offs


def _part_spec(block, offs, p, col_fn):
    lo, n = offs[p], offs[p + 1] - offs[p]
    return pl.BlockSpec(block, lambda i, j: (jnp.clip(i - lo, 0, n - 1), col_fn(i, j, lo, n)))


def _in_part(i, offs, p):
    return jnp.logical_and(i >= offs[p], i < offs[p + 1])


def _qkv_kernel(*refs, offs, n_q_tiles, n_rope_tiles):
    nparts = len(offs) - 1
    x_refs = refs[:nparts]
    nw_ref, w_ref, cos_ref, sin_ref, o_ref, h_ref = refs[nparts:]
    i = pl.program_id(0)
    j = pl.program_id(1)

    for p in range(nparts):
        @pl.when(jnp.logical_and(j == 0, _in_part(i, offs, p)))
        def _(p=p):
            h_ref[...] = _rms(x_refs[p][...], nw_ref[...]).astype(BF16)

    y = jnp.dot(h_ref[...], w_ref[...], preferred_element_type=F32)
    rope = j < n_rope_tiles
    c = jnp.where(rope, cos_ref[...], 1.0)
    s = jnp.where(rope, sin_ref[...], 0.0)
    scale = jnp.where(j < n_q_tiles, LOG2E * DA_HEAD_DIM ** -0.5, 1.0)
    c = c * scale
    s = s * scale
    for t in range(y.shape[1] // LANES):
        yt = y[:, t * LANES:(t + 1) * LANES]
        o_ref[:, t * LANES:(t + 1) * LANES] = (yt * c + pltpu.roll(yt, LANES // 2, 1) * s).astype(o_ref.dtype)


def _qkv_proj(x_parts, nw, w, cos_t, sin_t):
    d = x_parts[0].shape[1]
    t = sum(p.shape[0] for p in x_parts)
    n = w.shape[1]
    bm = _pick(math.gcd(*[p.shape[0] for p in x_parts]), 512)
    bn = _pick(d, 2048)
    offs = _part_offsets(x_parts, bm)
    return pl.pallas_call(
        functools.partial(_qkv_kernel, offs=tuple(offs), n_q_tiles=d // bn, n_rope_tiles=2 * d // bn),
        grid=(t // bm, n // bn),
        in_specs=[_part_spec((bm, d), offs, p, lambda i, j, lo, cnt: 0) for p in range(len(x_parts))] + [
            pl.BlockSpec((1, d), lambda i, j: (0, 0)),
            pl.BlockSpec((d, bn), lambda i, j: (0, j)),
            pl.BlockSpec((bm, LANES), lambda i, j: (i, 0)),
            pl.BlockSpec((bm, LANES), lambda i, j: (i, 0)),
        ],
        out_specs=pl.BlockSpec((bm, bn), lambda i, j: (i, j)),
        out_shape=jax.ShapeDtypeStruct((t, n), BF16),
        scratch_shapes=[pltpu.VMEM((bm, d), BF16)],
        compiler_params=_cparams(2),
        name="qkv_proj",
    )(*x_parts, nw, w, cos_t, sin_t)


def _attn_kernel(lam_ref, sub_ref, q_ref, k_ref, v_ref, o_ref, vt_ref, qt_ref, s_ref, smax_ref,
                 m_ref, acc_ref, *, tk, lambda_init):
    tq = q_ref.shape[0]
    nk = k_ref.shape[0] // tk

    @pl.when(pl.program_id(2) == 0)
    def _():
        def transpose_chunk(c, carry):
            start = pl.multiple_of(c * tk, tk)
            vt_ref[0:DA_V_DIM, pl.ds(start, tk)] = v_ref[pl.ds(start, tk), :].astype(F32).T.astype(BF16)
            return carry
        lax.fori_loop(0, nk, transpose_chunk, 0)
        extra = lax.broadcasted_iota(jnp.int32, (vt_ref.shape[0] - DA_V_DIM, vt_ref.shape[1]), 0)
        vt_ref[DA_V_DIM:, :] = jnp.where(extra == 0, 1.0, 0.0).astype(BF16)

    qt = q_ref[...].astype(F32).T
    row = lax.broadcasted_iota(jnp.int32, qt.shape, 0)
    in_map0 = (row % DA_HEAD_DIM) < (DA_HEAD_DIM // 2)
    qt_ref[:, 0:tq] = jnp.where(in_map0, qt, 0.0).astype(BF16)
    qt_ref[:, tq:2 * tq] = jnp.where(in_map0, 0.0, qt).astype(BF16)
    m_ref[...] = jnp.full(m_ref.shape, -jnp.inf, F32)
    acc_ref[...] = jnp.zeros(acc_ref.shape, F32)

    def tile_start(tile):
        return tile * tk if isinstance(tile, int) else pl.multiple_of(tile * tk, tk)

    def scores(tile, slot):
        k = k_ref[pl.ds(tile_start(tile), tk), :]
        st = jnp.dot(k, qt_ref[...], preferred_element_type=F32)
        s_ref[slot] = st
        smax_ref[slot] = jnp.max(st, axis=0, keepdims=True)

    def update(tile, slot):
        vt = vt_ref[:, pl.ds(tile_start(tile), tk)]
        m_old = m_ref[...]
        m_new = jnp.maximum(m_old, smax_ref[slot])
        alpha = jnp.exp2(m_old - m_new)
        p = jnp.exp2(s_ref[slot] - m_new)
        acc_ref[...] = alpha * acc_ref[...] + jnp.dot(vt, p.astype(BF16), preferred_element_type=F32)
        m_ref[...] = m_new

    scores(0, 0)

    def stage_pair(j):
        t = 2 * j
        scores(t + 1, 1)
        update(t, 0)
        scores(t + 2, 0)
        update(t + 1, 1)

    def body(i, carry):
        stage_pair(2 * i)
        stage_pair(2 * i + 1)
        return carry

    n_pairs = nk // 2 - 1
    lax.fori_loop(0, n_pairs // 2, body, 0)
    if n_pairs % 2:
        stage_pair(n_pairs - 1)
    scores(nk - 1, 1)
    update(nk - 2, 0)
    update(nk - 1, 1)

    lam_p = lam_ref[...]
    t1 = jnp.sum(lam_p[0:1, :] * lam_p[1:2, :], axis=-1, keepdims=True)
    t2 = jnp.sum(lam_p[2:3, :] * lam_p[3:4, :], axis=-1, keepdims=True)
    lam = jnp.exp(t1) - jnp.exp(t2) + lambda_init
    acc0, acc1 = acc_ref[:, 0:tq], acc_ref[:, tq:2 * tq]
    ot = (acc0[:DA_V_DIM] / acc0[DA_V_DIM:DA_V_DIM + 1]
          - lam * (acc1[:DA_V_DIM] / acc1[DA_V_DIM:DA_V_DIM + 1]))
    ms = jnp.mean(ot * ot, axis=0, keepdims=True)
    o = (ot * lax.rsqrt(ms + EPS)).T
    o_ref[...] = (o * sub_ref[...] * (1.0 - lambda_init)).astype(o_ref.dtype)


def _attn_alias_kernel(lam_ref, sub_ref, q_ref, k_ref, v_ref, prev_ref, o_ref, *scratch, **kw):
    del prev_ref
    _attn_kernel(lam_ref, sub_ref, q_ref, k_ref, v_ref, o_ref, *scratch, **kw)


def _attention_group(qkv, lam_p, sub_w, row0, nseq, s, lambda_init, prev_out):
    tq = _pick(s, 512)
    tk = _pick(s // 2, 1024)
    nq = s // tq
    assert row0 % s == 0 and (s // tk) % 2 == 0
    kw = dict(tk=tk, lambda_init=lambda_init)
    in_specs = [
        pl.BlockSpec((4, DA_HEAD_DIM), lambda b, h, i: (0, 0)),
        pl.BlockSpec((1, DA_V_DIM), lambda b, h, i: (0, 0)),
        pl.BlockSpec((tq, DA_V_DIM), lambda b, h, i: (row0 // tq + b * nq + i, h)),
        pl.BlockSpec((s, DA_V_DIM), lambda b, h, i: (row0 // s + b, DA_HEADS + h)),
        pl.BlockSpec((s, DA_V_DIM), lambda b, h, i: (row0 // s + b, 2 * DA_HEADS + h)),
    ]
    args = [lam_p, sub_w, qkv, qkv, qkv]
    if prev_out is None:
        kern, aliases = functools.partial(_attn_kernel, **kw), {}
    else:
        kern, aliases = functools.partial(_attn_alias_kernel, **kw), {len(args): 0}
        in_specs.append(pl.BlockSpec(memory_space=pl.ANY))
        args.append(prev_out)
    return pl.pallas_call(
        kern,
        grid=(nseq, DA_HEADS, nq),
        in_specs=in_specs,
        out_specs=pl.BlockSpec((tq, DA_V_DIM), lambda b, h, i: (row0 // tq + b * nq + i, h)),
        out_shape=jax.ShapeDtypeStruct((qkv.shape[0], DA_HEADS * DA_V_DIM), BF16),
        input_output_aliases=aliases,
        scratch_shapes=[
            pltpu.VMEM((DA_V_DIM + 16, s), BF16),
            pltpu.VMEM((DA_V_DIM, 2 * tq), BF16),
            pltpu.VMEM((2, tk, 2 * tq), F32),
            pltpu.VMEM((2, 1, 2 * tq), F32),
            pltpu.VMEM((1, 2 * tq), F32),
            pltpu.VMEM((DA_V_DIM + 16, 2 * tq), F32),
        ],
        compiler_params=_cparams(3),
        name="diff_attention",
    )(*args)


def _matmul_res_kernel(a_ref, w_ref, *refs, offs):
    r_refs, o_ref = refs[:-1], refs[-1]
    i = pl.program_id(0)
    y = jnp.dot(a_ref[...], w_ref[...], preferred_element_type=F32)
    for p, r_ref in enumerate(r_refs):
        @pl.when(_in_part(i, offs, p))
        def _(r_ref=r_ref):
            o_ref[...] = r_ref[...] + y


def _matmul_res(a, w, res_parts):
    t, k = a.shape
    n = w.shape[1]
    bm = _pick(math.gcd(*[p.shape[0] for p in res_parts]), 1024)
    bn = _pick(n, 1024)
    offs = _part_offsets(res_parts, bm)
    col = lambda i, j, lo, cnt: jnp.where(jnp.logical_and(i >= lo, i < lo + cnt), j, 0)
    return pl.pallas_call(
        functools.partial(_matmul_res_kernel, offs=tuple(offs)),
        grid=(t // bm, n // bn),
        in_specs=[
            pl.BlockSpec((bm, k), lambda i, j: (i, 0)),
            pl.BlockSpec((k, bn), lambda i, j: (0, j)),
        ] + [_part_spec((bm, bn), offs, p, col) for p in range(len(res_parts))],
        out_specs=pl.BlockSpec((bm, bn), lambda i, j: (i, j)),
        out_shape=jax.ShapeDtypeStruct((t, n), F32),
        compiler_params=_cparams(2),
        name="matmul_residual",
    )(a, w, *res_parts)


def _ffn_kernel(x_ref, nw_ref, wg_ref, wu_ref, wd_ref, nf_ref, *refs, offs, final_norm):
    o_refs, (h_ref, acc_ref) = refs[:-2], refs[-2:]
    i = pl.program_id(0)
    j = pl.program_id(1)

    @pl.when(j == 0)
    def _():
        h_ref[...] = _rms(x_ref[...], nw_ref[...]).astype(BF16)
        acc_ref[...] = jnp.zeros_like(acc_ref)

    h = h_ref[...]
    g = jnp.dot(h, wg_ref[...], preferred_element_type=F32)
    u = jnp.dot(h, wu_ref[...], preferred_element_type=F32)
    a = (_silu(g) * u).astype(BF16)
    acc_ref[...] += jnp.dot(a, wd_ref[...], preferred_element_type=F32)

    last = j == pl.num_programs(1) - 1
    for p, o_ref in enumerate(o_refs):
        @pl.when(jnp.logical_and(last, _in_part(i, offs, p)))
        def _(o_ref=o_ref):
            r = acc_ref[...] + x_ref[...]
            if final_norm:
                r = _rms(r, nf_ref[...])
            o_ref[...] = r


def _ffn(x, nw, w_gu, w_down, nf, final_norm, out_rows):
    t, d = x.shape
    dff = w_down.shape[0]
    bm = _pick(math.gcd(*out_rows), 512)
    bf = _pick(dff, 512)
    nf_tiles = dff // bf
    outs = [jax.ShapeDtypeStruct((r, d), F32) for r in out_rows]
    offs = _part_offsets(outs, bm)
    return pl.pallas_call(
        functools.partial(_ffn_kernel, offs=tuple(offs), final_norm=final_norm),
        grid=(t // bm, nf_tiles),
        in_specs=[
            pl.BlockSpec((bm, d), lambda i, j: (i, 0)),
            pl.BlockSpec((1, d), lambda i, j: (0, 0)),
            pl.BlockSpec((d, bf), lambda i, j: (0, j)),
            pl.BlockSpec((d, bf), lambda i, j: (0, j + nf_tiles)),
            pl.BlockSpec((bf, d), lambda i, j: (j, 0)),
            pl.BlockSpec((1, d), lambda i, j: (0, 0)),
        ],
        out_specs=[_part_spec((bm, d), offs, p, lambda i, j, lo, cnt: 0) for p in range(len(outs))],
        out_shape=outs,
        scratch_shapes=[pltpu.VMEM((bm, d), BF16), pltpu.VMEM((bm, d), F32)],
        compiler_params=_cparams(2),
        name="swiglu_ffn",
    )(x, nw, w_gu, w_gu, w_down, nf)


def _ssd_in_kernel(x_ref, nw_ref, w_ref, wdt_ref, o_ref, dt_ref, h_ref):
    j = pl.program_id(1)

    @pl.when(j == 0)
    def _():
        h = _rms(x_ref[...], nw_ref[...]).astype(BF16)
        h_ref[...] = h
        dt_ref[...] = jnp.dot(h, wdt_ref[...], preferred_element_type=F32)

    o_ref[...] = jnp.dot(h_ref[...], w_ref[...], preferred_element_type=F32).astype(o_ref.dtype)


def _ssd_in_proj(x, nw, w_zx, w_dt):
    t, d = x.shape
    n = w_zx.shape[1]
    ndt = w_dt.shape[1]
    bm = _pick(t, 1024)
    bn = _pick(n, 1024)
    return pl.pallas_call(
        _ssd_in_kernel,
        grid=(t // bm, n // bn),
        in_specs=[
            pl.BlockSpec((bm, d), lambda i, j: (i, 0)),
            pl.BlockSpec((1, d), lambda i, j: (0, 0)),
            pl.BlockSpec((d, bn), lambda i, j: (0, j)),
            pl.BlockSpec((d, ndt), lambda i, j: (0, 0)),
        ],
        out_specs=[
            pl.BlockSpec((bm, bn), lambda i, j: (i, j)),
            pl.BlockSpec((bm, ndt), lambda i, j: (i, 0)),
        ],
        out_shape=[jax.ShapeDtypeStruct((t, n), BF16), jax.ShapeDtypeStruct((t, ndt), F32)],
        scratch_shapes=[pltpu.VMEM((bm, d), BF16)],
        compiler_params=_cparams(2),
        name="ssd_in_proj",
    )(x, nw, w_zx, w_dt)


def _conv_kernel(flags_ref, prev_ref, cur_ref, next_ref, w_ref, b_ref, o_ref):
    f = flags_ref[pl.program_id(0)]
    at_start = (f & 1) == 1
    at_end = (f & 2) == 2
    x = cur_ref[...].astype(F32)
    bt = x.shape[0]
    row = lax.broadcasted_iota(jnp.int32, x.shape, 0)
    before = jnp.where(at_start, 0.0, prev_ref[SUBLANES - 1:SUBLANES, :].astype(F32))
    after = jnp.where(at_end, 0.0, next_ref[0:2, :].astype(F32))
    xm1 = jnp.where(row == 0, before, pltpu.roll(x, 1, 0))
    xp1 = jnp.where(row == bt - 1, after[0:1], pltpu.roll(x, bt - 1, 0))
    xp2 = jnp.where(row == bt - 2, after[0:1],
                    jnp.where(row == bt - 1, after[1:2], pltpu.roll(x, bt - 2, 0)))
    w = w_ref[...]
    y = xm1 * w[0:1] + x * w[1:2] + xp1 * w[2:3] + xp2 * w[3:4] + b_ref[...]
    o_ref[...] = _silu(y).astype(o_ref.dtype)


def _conv_silu(zx, conv_w, conv_b, seqs, col0):
    t = zx.shape[0]
    ncol = conv_w.shape[1]
    bt = _pick(math.gcd(*[s for _, s in seqs]), 512)
    bc = _pick(ncol, 1024)
    nblk = t // bt
    starts = {r for r, _ in seqs}
    ends = {r + s for r, s in seqs}
    flags = np.array([(1 if i * bt in starts else 0) | (2 if (i + 1) * bt in ends else 0)
                      for i in range(nblk)], np.int32)
    hb = bt // SUBLANES
    c0 = col0 // bc
    grid_spec = pltpu.PrefetchScalarGridSpec(
        num_scalar_prefetch=1,
        grid=(nblk, ncol // bc),
        in_specs=[
            pl.BlockSpec((SUBLANES, bc), lambda i, j, f: (jnp.maximum(i * hb - 1, 0), c0 + j)),
            pl.BlockSpec((bt, bc), lambda i, j, f: (i, c0 + j)),
            pl.BlockSpec((SUBLANES, bc), lambda i, j, f: (jnp.minimum((i + 1) * hb, t // SUBLANES - 1), c0 + j)),
            pl.BlockSpec((4, bc), lambda i, j, f: (0, j)),
            pl.BlockSpec((1, bc), lambda i, j, f: (0, j)),
        ],
        out_specs=pl.BlockSpec((bt, bc), lambda i, j, f: (i, j)),
    )
    return pl.pallas_call(
        _conv_kernel,
        grid_spec=grid_spec,
        out_shape=jax.ShapeDtypeStruct((t, ncol), BF16),
        compiler_params=_cparams(2),
        name="ssd_conv_silu",
    )(jnp.asarray(flags), zx, zx, zx, conv_w, conv_b)


def _scan_kernel(flags_ref, xs_f, b_f, c_f, dt_f, xs_b, b_b, c_b, dt_b, bias_ref, alog_ref, d_ref,
                 yf_ref, yb_ref, st_f, ut_f, wt_f, st_b, ut_b, wt_b):
    f = flags_ref[pl.program_id(0)]

    @pl.when((f & 1) == 1)
    def _():
        st_f[...] = jnp.zeros_like(st_f)

    @pl.when((f & 2) == 2)
    def _():
        st_b[...] = jnp.zeros_like(st_b)

    _scan_direction(xs_f, b_f, c_f, dt_f, bias_ref, alog_ref, d_ref, yf_ref, st_f, ut_f, wt_f, reverse=False)
    _scan_direction(xs_b, b_b, c_b, dt_b, bias_ref, alog_ref, d_ref, yb_ref, st_b, ut_b, wt_b, reverse=True)


def _scan_direction(xs_ref, b_ref, c_ref, dt_ref, bias_ref, alog_ref, d_ref, y_ref, state_ref, ut_ref,
                    wt_ref, *, reverse):
    q = SSD_CHUNK
    dir_off = SSD_HEADS if reverse else 0
    xdt_in = dt_ref[...] + bias_ref[...]
    dt = jnp.maximum(xdt_in, 0.0) + jnp.log(1.0 + jnp.exp(-jnp.abs(xdt_in)))
    da = dt * (-jnp.exp(alog_ref[...]))
    row = lax.broadcasted_iota(jnp.int32, (q, LANES), 0)
    cum = da
    k = 1
    while k < q:
        cum = cum + jnp.where(row >= k, pltpu.roll(cum, k, 0), 0.0)
        k *= 2
    tot = cum[q - 1:q, :]
    u = (tot + da - cum) if reverse else cum
    state_decay = jnp.exp(tot)
    u2 = u * LOG2E
    ut_ref[...] = (u2 - jnp.log2(dt)).T
    wt_ref[...] = (dt * jnp.exp(tot - u)).T

    li = lax.broadcasted_iota(jnp.int32, (q, q), 0)
    si = lax.broadcasted_iota(jnp.int32, (q, q), 1)
    allowed = (si >= li) if reverse else (li >= si)
    low = lax.broadcasted_iota(jnp.int32, (q, LANES), 1) < SSD_HEAD_DIM

    for g in range(SSD_GROUPS):
        bg = b_ref[:, g * SSD_D_STATE:(g + 1) * SSD_D_STATE]
        cg = c_ref[:, g * SSD_D_STATE:(g + 1) * SSD_D_STATE]
        bt = bg.astype(F32).T
        cb = jnp.dot(cg, bt.astype(BF16), preferred_element_type=F32)
        st = state_ref[g]
        y_off = jnp.dot(cg, st.astype(BF16), preferred_element_type=F32)
        for j in range(SSD_HPG // 2):
            cols = slice(g * SSD_GROUP_W + j * LANES, g * SSD_GROUP_W + (j + 1) * LANES)
            tile = slice(j * LANES, (j + 1) * LANES)
            att, inj, scale = [], [], []
            for r in (2 * j, 2 * j + 1):
                h = dir_off + g * SSD_HPG + r
                ucol = jnp.broadcast_to(u2[:, h:h + 1], (q, q))
                decay_dt = jnp.exp2(jnp.where(allowed, ucol - ut_ref[h:h + 1, :], -jnp.inf))
                att.append((cb * decay_dt).astype(BF16))
                inj.append((bt * wt_ref[h:h + 1, :]).astype(BF16))
                scale.append(jnp.exp2(ucol))
            xt = xs_ref[:, cols]
            zt = jnp.zeros_like(xt)
            rhs = jnp.concatenate([jnp.where(low, xt, zt), jnp.where(low, zt, xt)], axis=0)
            lhs = jnp.concatenate([jnp.concatenate(att, axis=1), jnp.concatenate(inj, axis=1)], axis=0)
            res = jnp.dot(lhs, rhs, preferred_element_type=F32)
            y = res[:q] + y_off[:, tile] * jnp.where(low, scale[0], scale[1])
            if not reverse:
                y = y + d_ref[:, cols] * xt.astype(F32)
            y_ref[:, cols] = y.astype(y_ref.dtype)
            h = dir_off + g * SSD_HPG + 2 * j
            keep = jnp.where(low[0:1], state_decay[:, h:h + 1], state_decay[:, h + 1:h + 2])
            state_ref[g, :, tile] = st[:, tile] * keep + res[q:]


def _ssd_scan(xbc, dt_raw, dt_bias, a_log, d_row, seqs):
    t = xbc.shape[0]
    q = SSD_CHUNK
    nchunk = t // q
    starts = {r // q for r, _ in seqs}
    ends = {(r + s) // q - 1 for r, s in seqs}
    flags = np.array([(1 if c in starts else 0) | (2 if nchunk - 1 - c in ends else 0)
                      for c in range(nchunk)], np.int32)
    fwd = lambda c: c
    bwd = lambda c: nchunk - 1 - c
    nxb = SSD_D_INNER // SSD_BC_W

    def chunk_specs(blk):
        return [
            pl.BlockSpec((q, SSD_D_INNER), lambda c, f: (blk(c), 0)),
            pl.BlockSpec((q, SSD_BC_W), lambda c, f: (blk(c), nxb)),
            pl.BlockSpec((q, SSD_BC_W), lambda c, f: (blk(c), nxb + 1)),
            pl.BlockSpec((q, 2 * SSD_HEADS), lambda c, f: (blk(c), 0)),
        ]

    per_direction = [pltpu.VMEM((SSD_GROUPS, SSD_D_STATE, SSD_GROUP_W), F32),
                     pltpu.VMEM((2 * SSD_HEADS, q), F32),
                     pltpu.VMEM((2 * SSD_HEADS, q), F32)]
    grid_spec = pltpu.PrefetchScalarGridSpec(
        num_scalar_prefetch=1,
        grid=(nchunk,),
        in_specs=chunk_specs(fwd) + chunk_specs(bwd) + [
            pl.BlockSpec((1, 2 * SSD_HEADS), lambda c, f: (0, 0)),
            pl.BlockSpec((1, 2 * SSD_HEADS), lambda c, f: (0, 0)),
            pl.BlockSpec((1, SSD_D_INNER), lambda c, f: (0, 0)),
        ],
        out_specs=[pl.BlockSpec((q, SSD_D_INNER), lambda c, f: (fwd(c), 0)),
                   pl.BlockSpec((q, SSD_D_INNER), lambda c, f: (bwd(c), 0))],
        scratch_shapes=per_direction + per_direction,
    )
    out = jax.ShapeDtypeStruct((t, SSD_D_INNER), BF16)
    return pl.pallas_call(
        _scan_kernel,
        grid_spec=grid_spec,
        out_shape=[out, out],
        compiler_params=_cparams(1),
        name="ssd_scan",
    )(jnp.asarray(flags), xbc, xbc, xbc, dt_raw, xbc, xbc, xbc, dt_raw, dt_bias, a_log, d_row)


def _ssd_out_kernel(yf_ref, yb_ref, z_ref, nw_ref, w_ref, r_ref, o_ref, yn_ref):
    j = pl.program_id(1)

    @pl.when(j == 0)
    def _():
        for g in range(SSD_GROUPS):
            cols = slice(g * SSD_GROUP_W, (g + 1) * SSD_GROUP_W)
            y = yf_ref[:, cols].astype(F32) + yb_ref[:, cols].astype(F32)
            gated = y * _silu(z_ref[:, cols].astype(F32))
            yn_ref[:, cols] = _rms(gated, nw_ref[:, cols]).astype(BF16)

    o_ref[...] = r_ref[...] + jnp.dot(yn_ref[...], w_ref[...], preferred_element_type=F32)


def _ssd_out_proj(y_fwd, y_bwd, zx, nw, w, res):
    t, k = y_fwd.shape
    n = w.shape[1]
    bm = _pick(t, 512)
    bn = _pick(n, 512)
    return pl.pallas_call(
        _ssd_out_kernel,
        grid=(t // bm, n // bn),
        in_specs=[
            pl.BlockSpec((bm, k), lambda i, j: (i, 0)),
            pl.BlockSpec((bm, k), lambda i, j: (i, 0)),
            pl.BlockSpec((bm, k), lambda i, j: (i, 0)),
            pl.BlockSpec((1, k), lambda i, j: (0, 0)),
            pl.BlockSpec((k, bn), lambda i, j: (0, j)),
            pl.BlockSpec((bm, bn), lambda i, j: (i, j)),
        ],
        out_specs=pl.BlockSpec((bm, bn), lambda i, j: (i, j)),
        out_shape=jax.ShapeDtypeStruct((t, n), F32),
        scratch_shapes=[pltpu.VMEM((bm, k), BF16)],
        compiler_params=_cparams(2),
        name="ssd_out_proj",
    )(y_fwd, y_bwd, zx, nw, w, res)


def _rope_tables(seqs, t):
    half = DA_HEAD_DIM // 2
    inv = 1.0 / (ROPE_THETA ** (jnp.arange(0, DA_HEAD_DIM, 2, dtype=F32) / DA_HEAD_DIM))
    pos = np.zeros((t,), np.float32)
    for r, s in seqs:
        pos[r:r + s] = np.arange(s, dtype=np.float32)
    ang = jnp.asarray(pos)[:, None] * inv[None, :]
    cos, sin = jnp.cos(ang), jnp.sin(ang)
    cos_t = jnp.concatenate([cos, cos, cos, cos], axis=1)
    sin_t = jnp.concatenate([-sin, -sin, sin, sin], axis=1)
    assert cos_t.shape == (t, LANES) and 4 * half == LANES
    return cos_t, sin_t


def _reorder_qk_columns(w_qkv):
    d = w_qkv.shape[0]
    half = DA_HEAD_DIM // 2
    qk = w_qkv[:, :2 * d].reshape(d, 2 * d // LANES, 2, 2, half)
    qk = qk.transpose(0, 1, 3, 2, 4).reshape(d, 2 * d)
    return jnp.concatenate([qk, w_qkv[:, 2 * d:]], axis=1)


def _seq_groups(seqs):
    groups = []
    for r, s in seqs:
        if groups and groups[-1][2] == s and groups[-1][0] + groups[-1][1] * s == r:
            groups[-1][1] += 1
        else:
            groups.append([r, 1, s])
    return [tuple(g) for g in groups]


def _attention_layer(x_parts, seqs, nw, w_qkv, w_o, lam_p, sub_w, lambda_init):
    t = sum(p.shape[0] for p in x_parts)
    cos_t, sin_t = _rope_tables(seqs, t)
    qkv = _qkv_proj(x_parts, nw, _reorder_qk_columns(w_qkv).astype(BF16), cos_t, sin_t)
    o = None
    for r, n, s in _seq_groups(seqs):
        o = _attention_group(qkv, lam_p, sub_w, r, n, s, lambda_init, o)
    return _matmul_res(o, w_o.astype(BF16), x_parts)


def _ssd_layer(x, seqs, nw, w_in, conv_w, conv_b, dt_bias, a_log, d_skip, norm_w, w_out):
    nzx = SSD_D_INNER + SSD_CONV_DIM
    zx, dt_raw = _ssd_in_proj(x, nw, w_in[:, :nzx].astype(BF16), w_in[:, nzx:].astype(BF16))
    xbc = _conv_silu(zx, conv_w, conv_b.reshape(1, -1), seqs, SSD_D_INNER)
    bias = dt_bias.reshape(1, -1)
    alog = a_log.reshape(1, -1)
    d_row = jnp.repeat(d_skip, SSD_HEAD_DIM).reshape(1, -1)
    y_fwd, y_bwd = _ssd_scan(xbc, dt_raw, bias, alog, d_row, seqs)
    return _ssd_out_proj(y_fwd, y_bwd, zx, norm_w.reshape(1, -1), w_out.astype(BF16), x)


def _trunk(x_parts, seqs, norm_mix, norm_ffn, norm_final, da_w_qkv, da_w_o, da_lambda_q1, da_lambda_k1,
           da_lambda_q2, da_lambda_k2, da_subln, ssd_w_in, ssd_conv_w, ssd_conv_b, ssd_dt_bias,
           ssd_a_log, ssd_d, ssd_norm, ssd_w_out, ffn_w_gu, ffn_w_down):
    depth = norm_mix.shape[0]
    nf = norm_final.reshape(1, -1)
    part_rows = [p.shape[0] for p in x_parts]
    x = x_parts
    for i in range(depth):
        j = i // 2
        nw = norm_mix[i].reshape(1, -1)
        if i % 2 == 0:
            lambda_init = 0.8 - 0.6 * math.exp(-0.3 * i)
            lam_p = jnp.stack([da_lambda_q1[j], da_lambda_k1[j], da_lambda_q2[j], da_lambda_k2[j]])
            y = _attention_layer(x, seqs, nw, da_w_qkv[j], da_w_o[j], lam_p,
                                 da_subln[j].reshape(1, -1), lambda_init)
        else:
            (x0,) = x
            y = _ssd_layer(x0, seqs, nw, ssd_w_in[j], ssd_conv_w[j], ssd_conv_b[j], ssd_dt_bias[j],
                           ssd_a_log[j], ssd_d[j], ssd_norm[j], ssd_w_out[j])
        last = i == depth - 1
        x = tuple(_ffn(y, norm_ffn[i].reshape(1, -1), ffn_w_gu[i].astype(BF16), ffn_w_down[i].astype(BF16),
                       nf, final_norm=last, out_rows=part_rows if last else [y.shape[0]]))
    return x


def kernel(x_prompt, x_sample, norm_mix, norm_ffn, norm_final, da_w_qkv, da_w_o, da_lambda_q1, da_lambda_k1, da_lambda_q2, da_lambda_k2, da_subln, ssd_w_in, ssd_conv_w, ssd_conv_b, ssd_dt_bias, ssd_a_log, ssd_d, ssd_norm, ssd_w_out, ffn_w_gu, ffn_w_down):
    d = x_prompt.shape[-1]
    seqs = []
    for arr in (x_prompt, x_sample):
        for _ in range(arr.shape[0]):
            seqs.append((sum(s for _, s in seqs), arr.shape[1]))
    x_parts = (x_prompt.reshape(-1, d), x_sample.reshape(-1, d))
    y_p, y_s = _trunk(x_parts, tuple(seqs), norm_mix, norm_ffn, norm_final, da_w_qkv, da_w_o,
                      da_lambda_q1, da_lambda_k1, da_lambda_q2, da_lambda_k2, da_subln, ssd_w_in,
                      ssd_conv_w, ssd_conv_b, ssd_dt_bias, ssd_a_log, ssd_d, ssd_norm, ssd_w_out,
                      ffn_w_gu, ffn_w_down)
    return y_p.reshape(x_prompt.shape), y_s.reshape(x_sample.shape)
```

```python
import functools
import math

import numpy as np
import jax
import jax.numpy as jnp
from jax import lax
from jax.experimental import pallas as pl
from jax.experimental.pallas import tpu as pltpu

F32 = jnp.float32
BF16 = jnp.bfloat16

EPS = 1e-5
LOG2E = 1.4426950408889634
LANES = 128
SUBLANES = 8
VMEM_LIMIT = 56 * 1024 * 1024

DA_HEADS = 16
DA_HEAD_DIM = 64
DA_V_DIM = 2 * DA_HEAD_DIM
ROPE_THETA = 10000.0

SSD_HEAD_DIM = 64
SSD_GROUPS = 8
SSD_HPG = 8
SSD_HEADS = SSD_GROUPS * SSD_HPG
SSD_D_STATE = 128
SSD_D_INNER = SSD_HEADS * SSD_HEAD_DIM
SSD_CHUNK = 128
SSD_GROUP_W = SSD_HPG * SSD_HEAD_DIM
SSD_BC_W = SSD_GROUPS * SSD_D_STATE
SSD_CONV_DIM = SSD_D_INNER + 2 * SSD_BC_W


def _cparams(n_axes):
    return pltpu.CompilerParams(dimension_semantics=("arbitrary",) * n_axes,
                                vmem_limit_bytes=VMEM_LIMIT)


def _rms(x, w):
    ms = jnp.mean(x * x, axis=-1, keepdims=True)
    return x * lax.rsqrt(ms + EPS) * w


def _silu(x):
    return x / (1.0 + jnp.exp(-x))


def _pick(n, pref):
    b = min(n, pref)
    while n % b:
        b //= 2
    return b


def _part_offsets(parts, bm):
    offs = [0]
    for p in parts:
        assert p.shape[0] % bm == 0
        offs.append(offs[-1] + p.shape[0] // bm)
    return offs


def _part_spec(block, offs, p, col_fn):
    lo, n = offs[p], offs[p + 1] - offs[p]
    return pl.BlockSpec(block, lambda i, j: (jnp.clip(i - lo, 0, n - 1), col_fn(i, j, lo, n)))


def _in_part(i, offs, p):
    return jnp.logical_and(i >= offs[p], i < offs[p + 1])


def _qkv_kernel(*refs, offs, n_q_tiles, n_rope_tiles):
    nparts = len(offs) - 1
    x_refs = refs[:nparts]
    nw_ref, w_ref, cos_ref, sin_ref, o_ref, h_ref = refs[nparts:]
    i = pl.program_id(0)
    j = pl.program_id(1)

    for p in range(nparts):
        @pl.when(jnp.logical_and(j == 0, _in_part(i, offs, p)))
        def _(p=p):
            h_ref[...] = _rms(x_refs[p][...], nw_ref[...]).astype(BF16)

    y = jnp.dot(h_ref[...], w_ref[...], preferred_element_type=F32)
    rope = j < n_rope_tiles
    c = jnp.where(rope, cos_ref[...], 1.0)
    s = jnp.where(rope, sin_ref[...], 0.0)
    scale = jnp.where(j < n_q_tiles, LOG2E * DA_HEAD_DIM ** -0.5, 1.0)
    c = c * scale
    s = s * scale
    for t in range(y.shape[1] // LANES):
        yt = y[:, t * LANES:(t + 1) * LANES]
        o_ref[:, t * LANES:(t + 1) * LANES] = (yt * c + pltpu.roll(yt, LANES // 2, 1) * s).astype(o_ref.dtype)


def _qkv_proj(x_parts, nw, w, cos_t, sin_t):
    d = x_parts[0].shape[1]
    t = sum(p.shape[0] for p in x_parts)
    n = w.shape[1]
    bm = _pick(math.gcd(*[p.shape[0] for p in x_parts]), 512)
    bn = _pick(d, 2048)
    offs = _part_offsets(x_parts, bm)
    return pl.pallas_call(
        functools.partial(_qkv_kernel, offs=tuple(offs), n_q_tiles=d // bn, n_rope_tiles=2 * d // bn),
        grid=(t // bm, n // bn),
        in_specs=[_part_spec((bm, d), offs, p, lambda i, j, lo, cnt: 0) for p in range(len(x_parts))] + [
            pl.BlockSpec((1, d), lambda i, j: (0, 0)),
            pl.BlockSpec((d, bn), lambda i, j: (0, j)),
            pl.BlockSpec((bm, LANES), lambda i, j: (i, 0)),
            pl.BlockSpec((bm, LANES), lambda i, j: (i, 0)),
        ],
        out_specs=pl.BlockSpec((bm, bn), lambda i, j: (i, j)),
        out_shape=jax.ShapeDtypeStruct((t, n), BF16),
        scratch_shapes=[pltpu.VMEM((bm, d), BF16)],
        compiler_params=_cparams(2),
        name="qkv_proj",
    )(*x_parts, nw, w, cos_t, sin_t)


def _attn_kernel(lam_ref, sub_ref, q_ref, k_ref, v_ref, o_ref, vt_ref, qt_ref, s_ref, smax_ref,
                 m_ref, acc_ref, *, tk, lambda_init):
    tq = q_ref.shape[0]
    nk = k_ref.shape[0] // tk

    @pl.when(pl.program_id(2) == 0)
    def _():
        def transpose_chunk(c, carry):
            start = pl.multiple_of(c * tk, tk)
            vt_ref[0:DA_V_DIM, pl.ds(start, tk)] = v_ref[pl.ds(start, tk), :].astype(F32).T.astype(BF16)
            return carry
        lax.fori_loop(0, nk, transpose_chunk, 0)
        extra = lax.broadcasted_iota(jnp.int32, (vt_ref.shape[0] - DA_V_DIM, vt_ref.shape[1]), 0)
        vt_ref[DA_V_DIM:, :] = jnp.where(extra == 0, 1.0, 0.0).astype(BF16)

    qt = q_ref[...].astype(F32).T
    row = lax.broadcasted_iota(jnp.int32, qt.shape, 0)
    in_map0 = (row % DA_HEAD_DIM) < (DA_HEAD_DIM // 2)
    qt_ref[0] = jnp.where(in_map0, qt, 0.0).astype(BF16)
    qt_ref[1] = jnp.where(in_map0, 0.0, qt).astype(BF16)
    m_ref[...] = jnp.full(m_ref.shape, -jnp.inf, F32)
    acc_ref[...] = jnp.zeros(acc_ref.shape, F32)

    def tile_start(tile):
        return tile * tk if isinstance(tile, int) else pl.multiple_of(tile * tk, tk)

    def scores(tile, slot):
        k = k_ref[pl.ds(tile_start(tile), tk), :]
        for mp in range(2):
            st = jnp.dot(k, qt_ref[mp], preferred_element_type=F32)
            s_ref[slot, mp] = st
            smax_ref[slot, mp] = jnp.max(st, axis=0, keepdims=True)

    def update(tile, slot):
        vt = vt_ref[:, pl.ds(tile_start(tile), tk)]
        for mp in range(2):
            m_old = m_ref[mp]
            m_new = jnp.maximum(m_old, smax_ref[slot, mp])
            alpha = jnp.exp2(m_old - m_new)
            p = jnp.exp2(s_ref[slot, mp] - m_new)
            acc_ref[mp] = alpha * acc_ref[mp] + jnp.dot(vt, p.astype(BF16), preferred_element_type=F32)
            m_ref[mp] = m_new

    scores(0, 0)

    def stage_pair(j):
        t = 2 * j
        scores(t + 1, 1)
        update(t, 0)
        scores(t + 2, 0)
        update(t + 1, 1)

    def body(i, carry):
        stage_pair(2 * i)
        stage_pair(2 * i + 1)
        return carry

    n_pairs = nk // 2 - 1
    lax.fori_loop(0, n_pairs // 2, body, 0)
    if n_pairs % 2:
        stage_pair(n_pairs - 1)
    scores(nk - 1, 1)
    update(nk - 2, 0)
    update(nk - 1, 1)

    lam_p = lam_ref[...]
    t1 = jnp.sum(lam_p[0:1, :] * lam_p[1:2, :], axis=-1, keepdims=True)
    t2 = jnp.sum(lam_p[2:3, :] * lam_p[3:4, :], axis=-1, keepdims=True)
    lam = jnp.exp(t1) - jnp.exp(t2) + lambda_init
    acc0, acc1 = acc_ref[0], acc_ref[1]
    ot = (acc0[:DA_V_DIM] / acc0[DA_V_DIM:DA_V_DIM + 1]
          - lam * (acc1[:DA_V_DIM] / acc1[DA_V_DIM:DA_V_DIM + 1]))
    ms = jnp.mean(ot * ot, axis=0, keepdims=True)
    o = (ot * lax.rsqrt(ms + EPS)).T
    o_ref[...] = (o * sub_ref[...] * (1.0 - lambda_init)).astype(o_ref.dtype)


def _attn_alias_kernel(lam_ref, sub_ref, q_ref, k_ref, v_ref, prev_ref, o_ref, *scratch, **kw):
    del prev_ref
    _attn_kernel(lam_ref, sub_ref, q_ref, k_ref, v_ref, o_ref, *scratch, **kw)


def _attention_group(qkv, lam_p, sub_w, row0, nseq, s, lambda_init, prev_out):
    tq = _pick(s, 512)
    tk = _pick(s // 2, 1024)
    nq = s // tq
    assert row0 % s == 0 and (s // tk) % 2 == 0
    kw = dict(tk=tk, lambda_init=lambda_init)
    in_specs = [
        pl.BlockSpec((4, DA_HEAD_DIM), lambda b, h, i: (0, 0)),
        pl.BlockSpec((1, DA_V_DIM), lambda b, h, i: (0, 0)),
        pl.BlockSpec((tq, DA_V_DIM), lambda b, h, i: (row0 // tq + b * nq + i, h)),
        pl.BlockSpec((s, DA_V_DIM), lambda b, h, i: (row0 // s + b, DA_HEADS + h)),
        pl.BlockSpec((s, DA_V_DIM), lambda b, h, i: (row0 // s + b, 2 * DA_HEADS + h)),
    ]
    args = [lam_p, sub_w, qkv, qkv, qkv]
    if prev_out is None:
        kern, aliases = functools.partial(_attn_kernel, **kw), {}
    else:
        kern, aliases = functools.partial(_attn_alias_kernel, **kw), {len(args): 0}
        in_specs.append(pl.BlockSpec(memory_space=pl.ANY))
        args.append(prev_out)
    return pl.pallas_call(
        kern,
        grid=(nseq, DA_HEADS, nq),
        in_specs=in_specs,
        out_specs=pl.BlockSpec((tq, DA_V_DIM), lambda b, h, i: (row0 // tq + b * nq + i, h)),
        out_shape=jax.ShapeDtypeStruct((qkv.shape[0], DA_HEADS * DA_V_DIM), BF16),
        input_output_aliases=aliases,
        scratch_shapes=[
            pltpu.VMEM((DA_V_DIM + 16, s), BF16),
            pltpu.VMEM((2, DA_V_DIM, tq), BF16),
            pltpu.VMEM((2, 2, tk, tq), F32),
            pltpu.VMEM((2, 2, 1, tq), F32),
            pltpu.VMEM((2, 1, tq), F32),
            pltpu.VMEM((2, DA_V_DIM + 16, tq), F32),
        ],
        compiler_params=_cparams(3),
        name="diff_attention",
    )(*args)


def _matmul_res_kernel(a_ref, w_ref, *refs, offs):
    r_refs, o_ref = refs[:-1], refs[-1]
    i = pl.program_id(0)
    y = jnp.dot(a_ref[...], w_ref[...], preferred_element_type=F32)
    for p, r_ref in enumerate(r_refs):
        @pl.when(_in_part(i, offs, p))
        def _(r_ref=r_ref):
            o_ref[...] = r_ref[...] + y


def _matmul_res(a, w, res_parts):
    t, k = a.shape
    n = w.shape[1]
    bm = _pick(math.gcd(*[p.shape[0] for p in res_parts]), 1024)
    bn = _pick(n, 1024)
    offs = _part_offsets(res_parts, bm)
    col = lambda i, j, lo, cnt: jnp.where(jnp.logical_and(i >= lo, i < lo + cnt), j, 0)
    return pl.pallas_call(
        functools.partial(_matmul_res_kernel, offs=tuple(offs)),
        grid=(t // bm, n // bn),
        in_specs=[
            pl.BlockSpec((bm, k), lambda i, j: (i, 0)),
            pl.BlockSpec((k, bn), lambda i, j: (0, j)),
        ] + [_part_spec((bm, bn), offs, p, col) for p in range(len(res_parts))],
        out_specs=pl.BlockSpec((bm, bn), lambda i, j: (i, j)),
        out_shape=jax.ShapeDtypeStruct((t, n), F32),
        compiler_params=_cparams(2),
        name="matmul_residual",
    )(a, w, *res_parts)


def _ffn_kernel(x_ref, nw_ref, wg_ref, wu_ref, wd_ref, nf_ref, *refs, offs, final_norm):
    o_refs, (h_ref, acc_ref) = refs[:-2], refs[-2:]
    i = pl.program_id(0)
    j = pl.program_id(1)

    @pl.when(j == 0)
    def _():
        h_ref[...] = _rms(x_ref[...], nw_ref[...]).astype(BF16)
        acc_ref[...] = jnp.zeros_like(acc_ref)

    h = h_ref[...]
    g = jnp.dot(h, wg_ref[...], preferred_element_type=F32)
    u = jnp.dot(h, wu_ref[...], preferred_element_type=F32)
    a = (_silu(g) * u).astype(BF16)
    acc_ref[...] += jnp.dot(a, wd_ref[...], preferred_element_type=F32)

    last = j == pl.num_programs(1) - 1
    for p, o_ref in enumerate(o_refs):
        @pl.when(jnp.logical_and(last, _in_part(i, offs, p)))
        def _(o_ref=o_ref):
            r = acc_ref[...] + x_ref[...]
            if final_norm:
                r = _rms(r, nf_ref[...])
            o_ref[...] = r


def _ffn(x, nw, w_gu, w_down, nf, final_norm, out_rows):
    t, d = x.shape
    dff = w_down.shape[0]
    bm = _pick(math.gcd(*out_rows), 512)
    bf = _pick(dff, 512)
    nf_tiles = dff // bf
    outs = [jax.ShapeDtypeStruct((r, d), F32) for r in out_rows]
    offs = _part_offsets(outs, bm)
    return pl.pallas_call(
        functools.partial(_ffn_kernel, offs=tuple(offs), final_norm=final_norm),
        grid=(t // bm, nf_tiles),
        in_specs=[
            pl.BlockSpec((bm, d), lambda i, j: (i, 0)),
            pl.BlockSpec((1, d), lambda i, j: (0, 0)),
            pl.BlockSpec((d, bf), lambda i, j: (0, j)),
            pl.BlockSpec((d, bf), lambda i, j: (0, j + nf_tiles)),
            pl.BlockSpec((bf, d), lambda i, j: (j, 0)),
            pl.BlockSpec((1, d), lambda i, j: (0, 0)),
        ],
        out_specs=[_part_spec((bm, d), offs, p, lambda i, j, lo, cnt: 0) for p in range(len(outs))],
        out_shape=outs,
        scratch_shapes=[pltpu.VMEM((bm, d), BF16), pltpu.VMEM((bm, d), F32)],
        compiler_params=_cparams(2),
        name="swiglu_ffn",
    )(x, nw, w_gu, w_gu, w_down, nf)


def _ssd_in_kernel(x_ref, nw_ref, w_ref, wdt_ref, o_ref, dt_ref, h_ref):
    j = pl.program_id(1)

    @pl.when(j == 0)
    def _():
        h = _rms(x_ref[...], nw_ref[...]).astype(BF16)
        h_ref[...] = h
        dt_ref[...] = jnp.dot(h, wdt_ref[...], preferred_element_type=F32)

    o_ref[...] = jnp.dot(h_ref[...], w_ref[...], preferred_element_type=F32).astype(o_ref.dtype)


def _ssd_in_proj(x, nw, w_zx, w_dt):
    t, d = x.shape
    n = w_zx.shape[1]
    ndt = w_dt.shape[1]
    bm = _pick(t, 1024)
    bn = _pick(n, 1024)
    return pl.pallas_call(
        _ssd_in_kernel,
        grid=(t // bm, n // bn),
        in_specs=[
            pl.BlockSpec((bm, d), lambda i, j: (i, 0)),
            pl.BlockSpec((1, d), lambda i, j: (0, 0)),
            pl.BlockSpec((d, bn), lambda i, j: (0, j)),
            pl.BlockSpec((d, ndt), lambda i, j: (0, 0)),
        ],
        out_specs=[
            pl.BlockSpec((bm, bn), lambda i, j: (i, j)),
            pl.BlockSpec((bm, ndt), lambda i, j: (i, 0)),
        ],
        out_shape=[jax.ShapeDtypeStruct((t, n), BF16), jax.ShapeDtypeStruct((t, ndt), F32)],
        scratch_shapes=[pltpu.VMEM((bm, d), BF16)],
        compiler_params=_cparams(2),
        name="ssd_in_proj",
    )(x, nw, w_zx, w_dt)


def _conv_kernel(flags_ref, prev_ref, cur_ref, next_ref, w_ref, b_ref, o_ref):
    f = flags_ref[pl.program_id(0)]
    at_start = (f & 1) == 1
    at_end = (f & 2) == 2
    x = cur_ref[...].astype(F32)
    bt = x.shape[0]
    row = lax.broadcasted_iota(jnp.int32, x.shape, 0)
    before = jnp.where(at_start, 0.0, prev_ref[SUBLANES - 1:SUBLANES, :].astype(F32))
    after = jnp.where(at_end, 0.0, next_ref[0:2, :].astype(F32))
    xm1 = jnp.where(row == 0, before, pltpu.roll(x, 1, 0))
    xp1 = jnp.where(row == bt - 1, after[0:1], pltpu.roll(x, bt - 1, 0))
    xp2 = jnp.where(row == bt - 2, after[0:1],
                    jnp.where(row == bt - 1, after[1:2], pltpu.roll(x, bt - 2, 0)))
    w = w_ref[...]
    y = xm1 * w[0:1] + x * w[1:2] + xp1 * w[2:3] + xp2 * w[3:4] + b_ref[...]
    o_ref[...] = _silu(y).astype(o_ref.dtype)


def _conv_silu(zx, conv_w, conv_b, seqs, col0):
    t = zx.shape[0]
    ncol = conv_w.shape[1]
    bt = _pick(math.gcd(*[s for _, s in seqs]), 512)
    bc = _pick(ncol, 1024)
    nblk = t // bt
    starts = {r for r, _ in seqs}
    ends = {r + s for r, s in seqs}
    flags = np.array([(1 if i * bt in starts else 0) | (2 if (i + 1) * bt in ends else 0)
                      for i in range(nblk)], np.int32)
    hb = bt // SUBLANES
    c0 = col0 // bc
    grid_spec = pltpu.PrefetchScalarGridSpec(
        num_scalar_prefetch=1,
        grid=(nblk, ncol // bc),
        in_specs=[
            pl.BlockSpec((SUBLANES, bc), lambda i, j, f: (jnp.maximum(i * hb - 1, 0), c0 + j)),
            pl.BlockSpec((bt, bc), lambda i, j, f: (i, c0 + j)),
            pl.BlockSpec((SUBLANES, bc), lambda i, j, f: (jnp.minimum((i + 1) * hb, t // SUBLANES - 1), c0 + j)),
            pl.BlockSpec((4, bc), lambda i, j, f: (0, j)),
            pl.BlockSpec((1, bc), lambda i, j, f: (0, j)),
        ],
        out_specs=pl.BlockSpec((bt, bc), lambda i, j, f: (i, j)),
    )
    return pl.pallas_call(
        _conv_kernel,
        grid_spec=grid_spec,
        out_shape=jax.ShapeDtypeStruct((t, ncol), BF16),
        compiler_params=_cparams(2),
        name="ssd_conv_silu",
    )(jnp.asarray(flags), zx, zx, zx, conv_w, conv_b)


def _scan_kernel(flags_ref, xs_f, b_f, c_f, dt_f, xs_b, b_b, c_b, dt_b, bias_ref, alog_ref, d_ref,
                 yf_ref, yb_ref, st_f, ut_f, wt_f, st_b, ut_b, wt_b):
    f = flags_ref[pl.program_id(0)]

    @pl.when((f & 1) == 1)
    def _():
        st_f[...] = jnp.zeros_like(st_f)

    @pl.when((f & 2) == 2)
    def _():
        st_b[...] = jnp.zeros_like(st_b)

    _scan_direction(xs_f, b_f, c_f, dt_f, bias_ref, alog_ref, d_ref, yf_ref, st_f, ut_f, wt_f, reverse=False)
    _scan_direction(xs_b, b_b, c_b, dt_b, bias_ref, alog_ref, d_ref, yb_ref, st_b, ut_b, wt_b, reverse=True)


def _scan_direction(xs_ref, b_ref, c_ref, dt_ref, bias_ref, alog_ref, d_ref, y_ref, state_ref, ut_ref,
                    wt_ref, *, reverse):
    q = SSD_CHUNK
    dir_off = SSD_HEADS if reverse else 0
    xdt_in = dt_ref[...] + bias_ref[...]
    dt = jnp.maximum(xdt_in, 0.0) + jnp.log(1.0 + jnp.exp(-jnp.abs(xdt_in)))
    da = dt * (-jnp.exp(alog_ref[...]))
    row = lax.broadcasted_iota(jnp.int32, (q, LANES), 0)
    cum = da
    k = 1
    while k < q:
        cum = cum + jnp.where(row >= k, pltpu.roll(cum, k, 0), 0.0)
        k *= 2
    tot = cum[q - 1:q, :]
    u = (tot + da - cum) if reverse else cum
    state_decay = jnp.exp(tot)
    u2 = u * LOG2E
    ut_ref[...] = (u2 - jnp.log2(dt)).T
    wt_ref[...] = (dt * jnp.exp(tot - u)).T

    li = lax.broadcasted_iota(jnp.int32, (q, q), 0)
    si = lax.broadcasted_iota(jnp.int32, (q, q), 1)
    allowed = (si >= li) if reverse else (li >= si)
    low = lax.broadcasted_iota(jnp.int32, (q, LANES), 1) < SSD_HEAD_DIM

    for g in range(SSD_GROUPS):
        bg = b_ref[:, g * SSD_D_STATE:(g + 1) * SSD_D_STATE]
        cg = c_ref[:, g * SSD_D_STATE:(g + 1) * SSD_D_STATE]
        bt = bg.astype(F32).T
        cb = jnp.dot(cg, bt.astype(BF16), preferred_element_type=F32)
        st = state_ref[g]
        y_off = jnp.dot(cg, st.astype(BF16), preferred_element_type=F32)
        for j in range(SSD_HPG // 2):
            cols = slice(g * SSD_GROUP_W + j * LANES, g * SSD_GROUP_W + (j + 1) * LANES)
            tile = slice(j * LANES, (j + 1) * LANES)
            att, inj, scale = [], [], []
            for r in (2 * j, 2 * j + 1):
                h = dir_off + g * SSD_HPG + r
                ucol = jnp.broadcast_to(u2[:, h:h + 1], (q, q))
                decay_dt = jnp.exp2(jnp.where(allowed, ucol - ut_ref[h:h + 1, :], -jnp.inf))
                att.append((cb * decay_dt).astype(BF16))
                inj.append((bt * wt_ref[h:h + 1, :]).astype(BF16))
                scale.append(jnp.exp2(ucol))
            xt = xs_ref[:, cols]
            zt = jnp.zeros_like(xt)
            rhs = jnp.concatenate([jnp.where(low, xt, zt), jnp.where(low, zt, xt)], axis=0)
            lhs = jnp.concatenate([jnp.concatenate(att, axis=1), jnp.concatenate(inj, axis=1)], axis=0)
            res = jnp.dot(lhs, rhs, preferred_element_type=F32)
            y = res[:q] + y_off[:, tile] * jnp.where(low, scale[0], scale[1])
            if not reverse:
                y = y + d_ref[:, cols] * xt.astype(F32)
            y_ref[:, cols] = y.astype(y_ref.dtype)
            h = dir_off + g * SSD_HPG + 2 * j
            keep = jnp.where(low[0:1], state_decay[:, h:h + 1], state_decay[:, h + 1:h + 2])
            state_ref[g, :, tile] = st[:, tile] * keep + res[q:]


def _ssd_scan(xbc, dt_raw, dt_bias, a_log, d_row, seqs):
    t = xbc.shape[0]
    q = SSD_CHUNK
    nchunk = t // q
    starts = {r // q for r, _ in seqs}
    ends = {(r + s) // q - 1 for r, s in seqs}
    flags = np.array([(1 if c in starts else 0) | (2 if nchunk - 1 - c in ends else 0)
                      for c in range(nchunk)], np.int32)
    fwd = lambda c: c
    bwd = lambda c: nchunk - 1 - c
    nxb = SSD_D_INNER // SSD_BC_W

    def chunk_specs(blk):
        return [
            pl.BlockSpec((q, SSD_D_INNER), lambda c, f: (blk(c), 0)),
            pl.BlockSpec((q, SSD_BC_W), lambda c, f: (blk(c), nxb)),
            pl.BlockSpec((q, SSD_BC_W), lambda c, f: (blk(c), nxb + 1)),
            pl.BlockSpec((q, 2 * SSD_HEADS), lambda c, f: (blk(c), 0)),
        ]

    per_direction = [pltpu.VMEM((SSD_GROUPS, SSD_D_STATE, SSD_GROUP_W), F32),
                     pltpu.VMEM((2 * SSD_HEADS, q), F32),
                     pltpu.VMEM((2 * SSD_HEADS, q), F32)]
    grid_spec = pltpu.PrefetchScalarGridSpec(
        num_scalar_prefetch=1,
        grid=(nchunk,),
        in_specs=chunk_specs(fwd) + chunk_specs(bwd) + [
            pl.BlockSpec((1, 2 * SSD_HEADS), lambda c, f: (0, 0)),
            pl.BlockSpec((1, 2 * SSD_HEADS), lambda c, f: (0, 0)),
            pl.BlockSpec((1, SSD_D_INNER), lambda c, f: (0, 0)),
        ],
        out_specs=[pl.BlockSpec((q, SSD_D_INNER), lambda c, f: (fwd(c), 0)),
                   pl.BlockSpec((q, SSD_D_INNER), lambda c, f: (bwd(c), 0))],
        scratch_shapes=per_direction + per_direction,
    )
    out = jax.ShapeDtypeStruct((t, SSD_D_INNER), BF16)
    return pl.pallas_call(
        _scan_kernel,
        grid_spec=grid_spec,
        out_shape=[out, out],
        compiler_params=_cparams(1),
        name="ssd_scan",
    )(jnp.asarray(flags), xbc, xbc, xbc, dt_raw, xbc, xbc, xbc, dt_raw, dt_bias, a_log, d_row)


def _ssd_out_kernel(yf_ref, yb_ref, z_ref, nw_ref, w_ref, r_ref, o_ref):
    acc = r_ref[...]
    for g in range(SSD_GROUPS):
        cols = slice(g * SSD_GROUP_W, (g + 1) * SSD_GROUP_W)
        y = yf_ref[:, cols].astype(F32) + yb_ref[:, cols].astype(F32)
        gated = y * _silu(z_ref[:, cols].astype(F32))
        yn = _rms(gated, nw_ref[:, cols]).astype(BF16)
        acc = acc + jnp.dot(yn, w_ref[cols, :], preferred_element_type=F32)
    o_ref[...] = acc


def _ssd_out_proj(y_fwd, y_bwd, zx, nw, w, res):
    t, k = y_fwd.shape
    n = w.shape[1]
    bm = _pick(t, 256)
    return pl.pallas_call(
        _ssd_out_kernel,
        grid=(t // bm,),
        in_specs=[
            pl.BlockSpec((bm, k), lambda i: (i, 0)),
            pl.BlockSpec((bm, k), lambda i: (i, 0)),
            pl.BlockSpec((bm, k), lambda i: (i, 0)),
            pl.BlockSpec((1, k), lambda i: (0, 0)),
            pl.BlockSpec((k, n), lambda i: (0, 0), pipeline_mode=pl.Buffered(1)),
            pl.BlockSpec((bm, n), lambda i: (i, 0)),
        ],
        out_specs=pl.BlockSpec((bm, n), lambda i: (i, 0)),
        out_shape=jax.ShapeDtypeStruct((t, n), F32),
        compiler_params=_cparams(1),
        name="ssd_out_proj",
    )(y_fwd, y_bwd, zx, nw, w, res)


def _rope_tables(seqs, t):
    half = DA_HEAD_DIM // 2
    inv = 1.0 / (ROPE_THETA ** (jnp.arange(0, DA_HEAD_DIM, 2, dtype=F32) / DA_HEAD_DIM))
    pos = np.zeros((t,), np.float32)
    for r, s in seqs:
        pos[r:r + s] = np.arange(s, dtype=np.float32)
    ang = jnp.asarray(pos)[:, None] * inv[None, :]
    cos, sin = jnp.cos(ang), jnp.sin(ang)
    cos_t = jnp.concatenate([cos, cos, cos, cos], axis=1)
    sin_t = jnp.concatenate([-sin, -sin, sin, sin], axis=1)
    assert cos_t.shape == (t, LANES) and 4 * half == LANES
    return cos_t, sin_t


def _reorder_qk_columns(w_qkv):
    d = w_qkv.shape[0]
    half = DA_HEAD_DIM // 2
    qk = w_qkv[:, :2 * d].reshape(d, 2 * d // LANES, 2, 2, half)
    qk = qk.transpose(0, 1, 3, 2, 4).reshape(d, 2 * d)
    return jnp.concatenate([qk, w_qkv[:, 2 * d:]], axis=1)


def _seq_groups(seqs):
    groups = []
    for r, s in seqs:
        if groups and groups[-1][2] == s and groups[-1][0] + groups[-1][1] * s == r:
            groups[-1][1] += 1
        else:
            groups.append([r, 1, s])
    return [tuple(g) for g in groups]


def _attention_layer(x_parts, seqs, nw, w_qkv, w_o, lam_p, sub_w, lambda_init):
    t = sum(p.shape[0] for p in x_parts)
    cos_t, sin_t = _rope_tables(seqs, t)
    qkv = _qkv_proj(x_parts, nw, _reorder_qk_columns(w_qkv).astype(BF16), cos_t, sin_t)
    o = None
    for r, n, s in _seq_groups(seqs):
        o = _attention_group(qkv, lam_p, sub_w, r, n, s, lambda_init, o)
    return _matmul_res(o, w_o.astype(BF16), x_parts)


def _ssd_layer(x, seqs, nw, w_in, conv_w, conv_b, dt_bias, a_log, d_skip, norm_w, w_out):
    nzx = SSD_D_INNER + SSD_CONV_DIM
    zx, dt_raw = _ssd_in_proj(x, nw, w_in[:, :nzx].astype(BF16), w_in[:, nzx:].astype(BF16))
    xbc = _conv_silu(zx, conv_w, conv_b.reshape(1, -1), seqs, SSD_D_INNER)
    bias = dt_bias.reshape(1, -1)
    alog = a_log.reshape(1, -1)
    d_row = jnp.repeat(d_skip, SSD_HEAD_DIM).reshape(1, -1)
    y_fwd, y_bwd = _ssd_scan(xbc, dt_raw, bias, alog, d_row, seqs)
    return _ssd_out_proj(y_fwd, y_bwd, zx, norm_w.reshape(1, -1), w_out.astype(BF16), x)


def _trunk(x_parts, seqs, norm_mix, norm_ffn, norm_final, da_w_qkv, da_w_o, da_lambda_q1, da_lambda_k1,
           da_lambda_q2, da_lambda_k2, da_subln, ssd_w_in, ssd_conv_w, ssd_conv_b, ssd_dt_bias,
           ssd_a_log, ssd_d, ssd_norm, ssd_w_out, ffn_w_gu, ffn_w_down):
    depth = norm_mix.shape[0]
    nf = norm_final.reshape(1, -1)
    part_rows = [p.shape[0] for p in x_parts]
    x = x_parts
    for i in range(depth):
        j = i // 2
        nw = norm_mix[i].reshape(1, -1)
        if i % 2 == 0:
            lambda_init = 0.8 - 0.6 * math.exp(-0.3 * i)
            lam_p = jnp.stack([da_lambda_q1[j], da_lambda_k1[j], da_lambda_q2[j], da_lambda_k2[j]])
            y = _attention_layer(x, seqs, nw, da_w_qkv[j], da_w_o[j], lam_p,
                                 da_subln[j].reshape(1, -1), lambda_init)
        else:
            (x0,) = x
            y = _ssd_layer(x0, seqs, nw, ssd_w_in[j], ssd_conv_w[j], ssd_conv_b[j], ssd_dt_bias[j],
                           ssd_a_log[j], ssd_d[j], ssd_norm[j], ssd_w_out[j])
        last = i == depth - 1
        x = tuple(_ffn(y, norm_ffn[i].reshape(1, -1), ffn_w_gu[i].astype(BF16), ffn_w_down[i].astype(BF16),
                       nf, final_norm=last, out_rows=part_rows if last else [y.shape[0]]))
    return x


def kernel(x_prompt, x_sample, norm_mix, norm_ffn, norm_final, da_w_qkv, da_w_o, da_lambda_q1, da_lambda_k1, da_lambda_q2, da_lambda_k2, da_subln, ssd_w_in, ssd_conv_w, ssd_conv_b, ssd_dt_bias, ssd_a_log, ssd_d, ssd_norm, ssd_w_out, ffn_w_gu, ffn_w_down):
    d = x_prompt.shape[-1]
    seqs = []
    for arr in (x_prompt, x_sample):
        for _ in range(arr.shape[0]):
            seqs.append((sum(s for _, s in seqs), arr.shape[1]))
    x_parts = (x_prompt.reshape(-1, d), x_sample.reshape(-1, d))
    y_p, y_s = _trunk(x_parts, tuple(seqs), norm_mix, norm_ffn, norm_final, da_w_qkv, da_w_o,
                      da_lambda_q1, da_lambda_k1, da_lambda_q2, da_lambda_k2, da_subln, ssd_w_in,
                      ssd_conv_w, ssd_conv_b, ssd_dt_bias, ssd_a_log, ssd_d, ssd_norm, ssd_w_out,
                      ffn_w_gu, ffn_w_down)
    return y_p.reshape(x_prompt.shape), y_s.reshape(x_sample.shape)
```

```python
import functools
import math

import numpy as np
import jax
import jax.numpy as jnp
from jax import lax
from jax.experimental import pallas as pl
from jax.experimental.pallas import tpu as pltpu

F32 = jnp.float32
BF16 = jnp.bfloat16

EPS = 1e-5
LOG2E = 1.4426950408889634
LANES = 128
SUBLANES = 8
VMEM_LIMIT = 56 * 1024 * 1024

DA_HEADS = 16
DA_HEAD_DIM = 64
DA_V_DIM = 2 * DA_HEAD_DIM
ROPE_THETA = 10000.0

SSD_HEAD_DIM = 64
SSD_GROUPS = 8
SSD_HPG = 8
SSD_HEADS = SSD_GROUPS * SSD_HPG
SSD_D_STATE = 128
SSD_D_INNER = SSD_HEADS * SSD_HEAD_DIM
SSD_CHUNK = 128
SSD_GROUP_W = SSD_HPG * SSD_HEAD_DIM
SSD_BC_W = SSD_GROUPS * SSD_D_STATE
SSD_CONV_DIM = SSD_D_INNER + 2 * SSD_BC_W


def _cparams(n_axes):
    return pltpu.CompilerParams(dimension_semantics=("arbitrary",) * n_axes,
                                vmem_limit_bytes=VMEM_LIMIT)


def _rms(x, w):
    ms = jnp.mean(x * x, axis=-1, keepdims=True)
    return x * lax.rsqrt(ms + EPS) * w


def _silu(x):
    return x / (1.0 + jnp.exp(-x))


def _pick(n, pref):
    b = min(n, pref)
    while n % b:
        b //= 2
    return b


def _part_offsets(parts, bm):
    offs = [0]
    for p in parts:
        assert p.shape[0] % bm == 0
        offs.append(offs[-1] + p.shape[0] // bm)
    return offs


def _part_spec(block, offs, p, col_fn):
    lo, n = offs[p], offs[p + 1] - offs[p]
    return pl.BlockSpec(block, lambda i, j: (jnp.clip(i - lo, 0, n - 1), col_fn(i, j, lo, n)))


def _in_part(i, offs, p):
    return jnp.logical_and(i >= offs[p], i < offs[p + 1])


def _qkv_kernel(*refs, offs, n_q_tiles, n_rope_tiles):
    nparts = len(offs) - 1
    x_refs = refs[:nparts]
    nw_ref, w_ref, cos_ref, sin_ref, o_ref, h_ref = refs[nparts:]
    i = pl.program_id(0)
    j = pl.program_id(1)

    for p in range(nparts):
        @pl.when(jnp.logical_and(j == 0, _in_part(i, offs, p)))
        def _(p=p):
            h_ref[...] = _rms(x_refs[p][...], nw_ref[...]).astype(BF16)

    y = jnp.dot(h_ref[...], w_ref[...], preferred_element_type=F32)
    rope = j < n_rope_tiles
    c = jnp.where(rope, cos_ref[...], 1.0)
    s = jnp.where(rope, sin_ref[...], 0.0)
    scale = jnp.where(j < n_q_tiles, LOG2E * DA_HEAD_DIM ** -0.5, 1.0)
    c = c * scale
    s = s * scale
    for t in range(y.shape[1] // LANES):
        yt = y[:, t * LANES:(t + 1) * LANES]
        o_ref[:, t * LANES:(t + 1) * LANES] = (yt * c + pltpu.roll(yt, LANES // 2, 1) * s).astype(o_ref.dtype)


def _qkv_proj(x_parts, nw, w, cos_t, sin_t):
    d = x_parts[0].shape[1]
    t = sum(p.shape[0] for p in x_parts)
    n = w.shape[1]
    bm = _pick(math.gcd(*[p.shape[0] for p in x_parts]), 512)
    bn = _pick(d, 2048)
    offs = _part_offsets(x_parts, bm)
    return pl.pallas_call(
        functools.partial(_qkv_kernel, offs=tuple(offs), n_q_tiles=d // bn, n_rope_tiles=2 * d // bn),
        grid=(t // bm, n // bn),
        in_specs=[_part_spec((bm, d), offs, p, lambda i, j, lo, cnt: 0) for p in range(len(x_parts))] + [
            pl.BlockSpec((1, d), lambda i, j: (0, 0)),
            pl.BlockSpec((d, bn), lambda i, j: (0, j)),
            pl.BlockSpec((bm, LANES), lambda i, j: (i, 0)),
            pl.BlockSpec((bm, LANES), lambda i, j: (i, 0)),
        ],
        out_specs=pl.BlockSpec((bm, bn), lambda i, j: (i, j)),
        out_shape=jax.ShapeDtypeStruct((t, n), BF16),
        scratch_shapes=[pltpu.VMEM((bm, d), BF16)],
        compiler_params=_cparams(2),
        name="qkv_proj",
    )(*x_parts, nw, w, cos_t, sin_t)


def _attn_kernel(lam_ref, sub_ref, q_ref, k_ref, v_ref, o_ref, vt_ref, qt_ref, s_ref, smax_ref,
                 m_ref, acc_ref, *, tk, tq, lambda_init):
    n_sub = q_ref.shape[0] // tq
    nk = k_ref.shape[0] // tk

    @pl.when(pl.program_id(2) == 0)
    def _():
        def transpose_chunk(c, carry):
            start = pl.multiple_of(c * tk, tk)
            vt_ref[0:DA_V_DIM, pl.ds(start, tk)] = v_ref[pl.ds(start, tk), :].astype(F32).T.astype(BF16)
            return carry
        lax.fori_loop(0, nk, transpose_chunk, 0)
        extra = lax.broadcasted_iota(jnp.int32, (vt_ref.shape[0] - DA_V_DIM, vt_ref.shape[1]), 0)
        vt_ref[DA_V_DIM:, :] = jnp.where(extra == 0, 1.0, 0.0).astype(BF16)

    row = lax.broadcasted_iota(jnp.int32, (DA_V_DIM, tq), 0)
    in_map0 = (row % DA_HEAD_DIM) < (DA_HEAD_DIM // 2)
    for b in range(n_sub):
        qt = q_ref[b * tq:(b + 1) * tq, :].astype(F32).T
        qt_ref[b, 0] = jnp.where(in_map0, qt, 0.0).astype(BF16)
        qt_ref[b, 1] = jnp.where(in_map0, 0.0, qt).astype(BF16)
    m_ref[...] = jnp.full(m_ref.shape, -jnp.inf, F32)
    acc_ref[...] = jnp.zeros(acc_ref.shape, F32)

    def tile_start(tile):
        return tile * tk if isinstance(tile, int) else pl.multiple_of(tile * tk, tk)

    def scores(b, tile, slot):
        k = k_ref[pl.ds(tile_start(tile), tk), :]
        for mp in range(2):
            st = jnp.dot(k, qt_ref[b, mp], preferred_element_type=F32)
            s_ref[slot, mp] = st
            smax_ref[slot, mp] = jnp.max(st, axis=0, keepdims=True)

    def update(b, tile, slot):
        vt = vt_ref[:, pl.ds(tile_start(tile), tk)]
        for mp in range(2):
            m_old = m_ref[b, mp]
            m_new = jnp.maximum(m_old, smax_ref[slot, mp])
            alpha = jnp.exp2(m_old - m_new)
            p = jnp.exp2(s_ref[slot, mp] - m_new)
            acc_ref[b, mp] = alpha * acc_ref[b, mp] + jnp.dot(vt, p.astype(BF16), preferred_element_type=F32)
            m_ref[b, mp] = m_new

    if n_sub == 1:
        scores(0, 0, 0)

        def stage_pair(j):
            t = 2 * j
            scores(0, t + 1, 1)
            update(0, t, 0)
            scores(0, t + 2, 0)
            update(0, t + 1, 1)

        def body(i, carry):
            stage_pair(2 * i)
            stage_pair(2 * i + 1)
            return carry

        n_pairs = nk // 2 - 1
        lax.fori_loop(0, n_pairs // 2, body, 0)
        if n_pairs % 2:
            stage_pair(n_pairs - 1)
        scores(0, nk - 1, 1)
        update(0, nk - 2, 0)
        update(0, nk - 1, 1)
    else:
        stages = [(b, t) for b in range(n_sub) for t in range(nk)]
        scores(*stages[0], 0)
        for i in range(len(stages) - 1):
            scores(*stages[i + 1], (i + 1) % 2)
            update(*stages[i], i % 2)
        update(*stages[-1], (len(stages) - 1) % 2)

    lam_p = lam_ref[...]
    t1 = jnp.sum(lam_p[0:1, :] * lam_p[1:2, :], axis=-1, keepdims=True)
    t2 = jnp.sum(lam_p[2:3, :] * lam_p[3:4, :], axis=-1, keepdims=True)
    lam = jnp.exp(t1) - jnp.exp(t2) + lambda_init
    for b in range(n_sub):
        acc0, acc1 = acc_ref[b, 0], acc_ref[b, 1]
        ot = (acc0[:DA_V_DIM] / acc0[DA_V_DIM:DA_V_DIM + 1]
              - lam * (acc1[:DA_V_DIM] / acc1[DA_V_DIM:DA_V_DIM + 1]))
        ms = jnp.mean(ot * ot, axis=0, keepdims=True)
        o = (ot * lax.rsqrt(ms + EPS)).T
        o_ref[b * tq:(b + 1) * tq, :] = (o * sub_ref[...] * (1.0 - lambda_init)).astype(o_ref.dtype)


def _attn_alias_kernel(lam_ref, sub_ref, q_ref, k_ref, v_ref, prev_ref, o_ref, *scratch, **kw):
    del prev_ref
    _attn_kernel(lam_ref, sub_ref, q_ref, k_ref, v_ref, o_ref, *scratch, **kw)


def _attention_group(qkv, lam_p, sub_w, row0, nseq, s, lambda_init, prev_out):
    tq = _pick(s, 512)
    tk = _pick(s // 2, 1024)
    n_sub = s // tq if (s // tq) * (s // tk) <= 8 else 1
    bq = n_sub * tq
    nq = s // bq
    assert row0 % s == 0 and (s // tk) % 2 == 0
    kw = dict(tk=tk, tq=tq, lambda_init=lambda_init)
    in_specs = [
        pl.BlockSpec((4, DA_HEAD_DIM), lambda b, h, i: (0, 0)),
        pl.BlockSpec((1, DA_V_DIM), lambda b, h, i: (0, 0)),
        pl.BlockSpec((bq, DA_V_DIM), lambda b, h, i: (row0 // bq + b * nq + i, h)),
        pl.BlockSpec((s, DA_V_DIM), lambda b, h, i: (row0 // s + b, DA_HEADS + h)),
        pl.BlockSpec((s, DA_V_DIM), lambda b, h, i: (row0 // s + b, 2 * DA_HEADS + h)),
    ]
    args = [lam_p, sub_w, qkv, qkv, qkv]
    if prev_out is None:
        kern, aliases = functools.partial(_attn_kernel, **kw), {}
    else:
        kern, aliases = functools.partial(_attn_alias_kernel, **kw), {len(args): 0}
        in_specs.append(pl.BlockSpec(memory_space=pl.ANY))
        args.append(prev_out)
    return pl.pallas_call(
        kern,
        grid=(nseq, DA_HEADS, nq),
        in_specs=in_specs,
        out_specs=pl.BlockSpec((bq, DA_V_DIM), lambda b, h, i: (row0 // bq + b * nq + i, h)),
        out_shape=jax.ShapeDtypeStruct((qkv.shape[0], DA_HEADS * DA_V_DIM), BF16),
        input_output_aliases=aliases,
        scratch_shapes=[
            pltpu.VMEM((DA_V_DIM + 16, s), BF16),
            pltpu.VMEM((n_sub, 2, DA_V_DIM, tq), BF16),
            pltpu.VMEM((2, 2, tk, tq), F32),
            pltpu.VMEM((2, 2, 1, tq), F32),
            pltpu.VMEM((n_sub, 2, 1, tq), F32),
            pltpu.VMEM((n_sub, 2, DA_V_DIM + 16, tq), F32),
        ],
        compiler_params=_cparams(3),
        name="diff_attention",
    )(*args)


def _matmul_res_kernel(a_ref, w_ref, *refs, offs):
    r_refs, o_ref = refs[:-1], refs[-1]
    i = pl.program_id(0)
    y = jnp.dot(a_ref[...], w_ref[...], preferred_element_type=F32)
    for p, r_ref in enumerate(r_refs):
        @pl.when(_in_part(i, offs, p))
        def _(r_ref=r_ref):
            o_ref[...] = r_ref[...] + y


def _matmul_res(a, w, res_parts):
    t, k = a.shape
    n = w.shape[1]
    bm = _pick(math.gcd(*[p.shape[0] for p in res_parts]), 1024)
    bn = _pick(n, 1024)
    offs = _part_offsets(res_parts, bm)
    col = lambda i, j, lo, cnt: jnp.where(jnp.logical_and(i >= lo, i < lo + cnt), j, 0)
    return pl.pallas_call(
        functools.partial(_matmul_res_kernel, offs=tuple(offs)),
        grid=(t // bm, n // bn),
        in_specs=[
            pl.BlockSpec((bm, k), lambda i, j: (i, 0)),
            pl.BlockSpec((k, bn), lambda i, j: (0, j)),
        ] + [_part_spec((bm, bn), offs, p, col) for p in range(len(res_parts))],
        out_specs=pl.BlockSpec((bm, bn), lambda i, j: (i, j)),
        out_shape=jax.ShapeDtypeStruct((t, n), F32),
        compiler_params=_cparams(2),
        name="matmul_residual",
    )(a, w, *res_parts)


def _ffn_kernel(x_ref, nw_ref, wg_ref, wu_ref, wd_ref, nf_ref, *refs, offs, final_norm):
    o_refs, (h_ref, acc_ref) = refs[:-2], refs[-2:]
    i = pl.program_id(0)
    j = pl.program_id(1)

    @pl.when(j == 0)
    def _():
        h_ref[...] = _rms(x_ref[...], nw_ref[...]).astype(BF16)
        acc_ref[...] = jnp.zeros_like(acc_ref)

    h = h_ref[...]
    g = jnp.dot(h, wg_ref[...], preferred_element_type=F32)
    u = jnp.dot(h, wu_ref[...], preferred_element_type=F32)
    a = (_silu(g) * u).astype(BF16)
    acc_ref[...] += jnp.dot(a, wd_ref[...], preferred_element_type=F32)

    last = j == pl.num_programs(1) - 1
    for p, o_ref in enumerate(o_refs):
        @pl.when(jnp.logical_and(last, _in_part(i, offs, p)))
        def _(o_ref=o_ref):
            r = acc_ref[...] + x_ref[...]
            if final_norm:
                r = _rms(r, nf_ref[...])
            o_ref[...] = r


def _ffn(x, nw, w_gu, w_down, nf, final_norm, out_rows):
    t, d = x.shape
    dff = w_down.shape[0]
    bm = _pick(math.gcd(*out_rows), 512)
    bf = _pick(dff, 512)
    nf_tiles = dff // bf
    outs = [jax.ShapeDtypeStruct((r, d), F32) for r in out_rows]
    offs = _part_offsets(outs, bm)
    return pl.pallas_call(
        functools.partial(_ffn_kernel, offs=tuple(offs), final_norm=final_norm),
        grid=(t // bm, nf_tiles),
        in_specs=[
            pl.BlockSpec((bm, d), lambda i, j: (i, 0)),
            pl.BlockSpec((1, d), lambda i, j: (0, 0)),
            pl.BlockSpec((d, bf), lambda i, j: (0, j)),
            pl.BlockSpec((d, bf), lambda i, j: (0, j + nf_tiles)),
            pl.BlockSpec((bf, d), lambda i, j: (j, 0)),
            pl.BlockSpec((1, d), lambda i, j: (0, 0)),
        ],
        out_specs=[_part_spec((bm, d), offs, p, lambda i, j, lo, cnt: 0) for p in range(len(outs))],
        out_shape=outs,
        scratch_shapes=[pltpu.VMEM((bm, d), BF16), pltpu.VMEM((bm, d), F32)],
        compiler_params=_cparams(2),
        name="swiglu_ffn",
    )(x, nw, w_gu, w_gu, w_down, nf)


def _ssd_in_kernel(x_ref, nw_ref, w_ref, wdt_ref, o_ref, dt_ref, h_ref):
    j = pl.program_id(1)

    @pl.when(j == 0)
    def _():
        h = _rms(x_ref[...], nw_ref[...]).astype(BF16)
        h_ref[...] = h
        dt_ref[...] = jnp.dot(h, wdt_ref[...], preferred_element_type=F32)

    o_ref[...] = jnp.dot(h_ref[...], w_ref[...], preferred_element_type=F32).astype(o_ref.dtype)


def _ssd_in_proj(x, nw, w_zx, w_dt):
    t, d = x.shape
    n = w_zx.shape[1]
    ndt = w_dt.shape[1]
    bm = _pick(t, 1024)
    bn = _pick(n, 1024)
    return pl.pallas_call(
        _ssd_in_kernel,
        grid=(t // bm, n // bn),
        in_specs=[
            pl.BlockSpec((bm, d), lambda i, j: (i, 0)),
            pl.BlockSpec((1, d), lambda i, j: (0, 0)),
            pl.BlockSpec((d, bn), lambda i, j: (0, j)),
            pl.BlockSpec((d, ndt), lambda i, j: (0, 0)),
        ],
        out_specs=[
            pl.BlockSpec((bm, bn), lambda i, j: (i, j)),
            pl.BlockSpec((bm, ndt), lambda i, j: (i, 0)),
        ],
        out_shape=[jax.ShapeDtypeStruct((t, n), BF16), jax.ShapeDtypeStruct((t, ndt), F32)],
        scratch_shapes=[pltpu.VMEM((bm, d), BF16)],
        compiler_params=_cparams(2),
        name="ssd_in_proj",
    )(x, nw, w_zx, w_dt)


def _conv_kernel(flags_ref, prev_ref, cur_ref, next_ref, w_ref, b_ref, o_ref):
    f = flags_ref[pl.program_id(0)]
    at_start = (f & 1) == 1
    at_end = (f & 2) == 2
    x = cur_ref[...].astype(F32)
    bt = x.shape[0]
    row = lax.broadcasted_iota(jnp.int32, x.shape, 0)
    before = jnp.where(at_start, 0.0, prev_ref[SUBLANES - 1:SUBLANES, :].astype(F32))
    after = jnp.where(at_end, 0.0, next_ref[0:2, :].astype(F32))
    xm1 = jnp.where(row == 0, before, pltpu.roll(x, 1, 0))
    xp1 = jnp.where(row == bt - 1, after[0:1], pltpu.roll(x, bt - 1, 0))
    xp2 = jnp.where(row == bt - 2, after[0:1],
                    jnp.where(row == bt - 1, after[1:2], pltpu.roll(x, bt - 2, 0)))
    w = w_ref[...]
    y = xm1 * w[0:1] + x * w[1:2] + xp1 * w[2:3] + xp2 * w[3:4] + b_ref[...]
    o_ref[...] = _silu(y).astype(o_ref.dtype)


def _conv_silu(zx, conv_w, conv_b, seqs, col0):
    t = zx.shape[0]
    ncol = conv_w.shape[1]
    bt = _pick(math.gcd(*[s for _, s in seqs]), 512)
    bc = _pick(ncol, 1024)
    nblk = t // bt
    starts = {r for r, _ in seqs}
    ends = {r + s for r, s in seqs}
    flags = np.array([(1 if i * bt in starts else 0) | (2 if (i + 1) * bt in ends else 0)
                      for i in range(nblk)], np.int32)
    hb = bt // SUBLANES
    c0 = col0 // bc
    grid_spec = pltpu.PrefetchScalarGridSpec(
        num_scalar_prefetch=1,
        grid=(nblk, ncol // bc),
        in_specs=[
            pl.BlockSpec((SUBLANES, bc), lambda i, j, f: (jnp.maximum(i * hb - 1, 0), c0 + j)),
            pl.BlockSpec((bt, bc), lambda i, j, f: (i, c0 + j)),
            pl.BlockSpec((SUBLANES, bc), lambda i, j, f: (jnp.minimum((i + 1) * hb, t // SUBLANES - 1), c0 + j)),
            pl.BlockSpec((4, bc), lambda i, j, f: (0, j)),
            pl.BlockSpec((1, bc), lambda i, j, f: (0, j)),
        ],
        out_specs=pl.BlockSpec((bt, bc), lambda i, j, f: (i, j)),
    )
    return pl.pallas_call(
        _conv_kernel,
        grid_spec=grid_spec,
        out_shape=jax.ShapeDtypeStruct((t, ncol), BF16),
        compiler_params=_cparams(2),
        name="ssd_conv_silu",
    )(jnp.asarray(flags), zx, zx, zx, conv_w, conv_b)


def _scan_kernel(flags_ref, xs_f, b_f, c_f, dt_f, xs_b, b_b, c_b, dt_b, bias_ref, alog_ref, d_ref,
                 yf_ref, yb_ref, st_f, ut_f, wt_f, st_b, ut_b, wt_b):
    f = flags_ref[pl.program_id(0)]

    @pl.when((f & 1) == 1)
    def _():
        st_f[...] = jnp.zeros_like(st_f)

    @pl.when((f & 2) == 2)
    def _():
        st_b[...] = jnp.zeros_like(st_b)

    _scan_direction(xs_f, b_f, c_f, dt_f, bias_ref, alog_ref, d_ref, yf_ref, st_f, ut_f, wt_f, reverse=False)
    _scan_direction(xs_b, b_b, c_b, dt_b, bias_ref, alog_ref, d_ref, yb_ref, st_b, ut_b, wt_b, reverse=True)


def _scan_direction(xs_ref, b_ref, c_ref, dt_ref, bias_ref, alog_ref, d_ref, y_ref, state_ref, ut_ref,
                    wt_ref, *, reverse):
    q = SSD_CHUNK
    dir_off = SSD_HEADS if reverse else 0
    xdt_in = dt_ref[...] + bias_ref[...]
    dt = jnp.maximum(xdt_in, 0.0) + jnp.log(1.0 + jnp.exp(-jnp.abs(xdt_in)))
    da = dt * (-jnp.exp(alog_ref[...]))
    row = lax.broadcasted_iota(jnp.int32, (q, LANES), 0)
    cum = da
    k = 1
    while k < q:
        cum = cum + jnp.where(row >= k, pltpu.roll(cum, k, 0), 0.0)
        k *= 2
    tot = cum[q - 1:q, :]
    u = (tot + da - cum) if reverse else cum
    state_decay = jnp.exp(tot)
    u2 = u * LOG2E
    ut_ref[...] = (u2 - jnp.log2(dt)).T
    wt_ref[...] = (dt * jnp.exp(tot - u)).T

    li = lax.broadcasted_iota(jnp.int32, (q, q), 0)
    si = lax.broadcasted_iota(jnp.int32, (q, q), 1)
    allowed = (si >= li) if reverse else (li >= si)
    low = lax.broadcasted_iota(jnp.int32, (q, LANES), 1) < SSD_HEAD_DIM

    for g in range(SSD_GROUPS):
        bg = b_ref[:, g * SSD_D_STATE:(g + 1) * SSD_D_STATE]
        cg = c_ref[:, g * SSD_D_STATE:(g + 1) * SSD_D_STATE]
        bt = bg.astype(F32).T
        cb = jnp.dot(cg, bt.astype(BF16), preferred_element_type=F32)
        st = state_ref[g]
        y_off = jnp.dot(cg, st.astype(BF16), preferred_element_type=F32)
        for j in range(SSD_HPG // 2):
            cols = slice(g * SSD_GROUP_W + j * LANES, g * SSD_GROUP_W + (j + 1) * LANES)
            tile = slice(j * LANES, (j + 1) * LANES)
            att, inj, scale = [], [], []
            for r in (2 * j, 2 * j + 1):
                h = dir_off + g * SSD_HPG + r
                ucol = jnp.broadcast_to(u2[:, h:h + 1], (q, q))
                decay_dt = jnp.exp2(jnp.where(allowed, ucol - ut_ref[h:h + 1, :], -jnp.inf))
                att.append((cb * decay_dt).astype(BF16))
                inj.append((bt * wt_ref[h:h + 1, :]).astype(BF16))
                scale.append(jnp.exp2(ucol))
            xt = xs_ref[:, cols]
            zt = jnp.zeros_like(xt)
            rhs = jnp.concatenate([jnp.where(low, xt, zt), jnp.where(low, zt, xt)], axis=0)
            lhs = jnp.concatenate([jnp.concatenate(att, axis=1), jnp.concatenate(inj, axis=1)], axis=0)
            res = jnp.dot(lhs, rhs, preferred_element_type=F32)
            y = res[:q] + y_off[:, tile] * jnp.where(low, scale[0], scale[1])
            if not reverse:
                y = y + d_ref[:, cols] * xt.astype(F32)
            y_ref[:, cols] = y.astype(y_ref.dtype)
            h = dir_off + g * SSD_HPG + 2 * j
            keep = jnp.where(low[0:1], state_decay[:, h:h + 1], state_decay[:, h + 1:h + 2])
            state_ref[g, :, tile] = st[:, tile] * keep + res[q:]


def _ssd_scan(xbc, dt_raw, dt_bias, a_log, d_row, seqs):
    t = xbc.shape[0]
    q = SSD_CHUNK
    nchunk = t // q
    starts = {r // q for r, _ in seqs}
    ends = {(r + s) // q - 1 for r, s in seqs}
    flags = np.array([(1 if c in starts else 0) | (2 if nchunk - 1 - c in ends else 0)
                      for c in range(nchunk)], np.int32)
    fwd = lambda c: c
    bwd = lambda c: nchunk - 1 - c
    nxb = SSD_D_INNER // SSD_BC_W

    def chunk_specs(blk):
        return [
            pl.BlockSpec((q, SSD_D_INNER), lambda c, f: (blk(c), 0)),
            pl.BlockSpec((q, SSD_BC_W), lambda c, f: (blk(c), nxb)),
            pl.BlockSpec((q, SSD_BC_W), lambda c, f: (blk(c), nxb + 1)),
            pl.BlockSpec((q, 2 * SSD_HEADS), lambda c, f: (blk(c), 0)),
        ]

    per_direction = [pltpu.VMEM((SSD_GROUPS, SSD_D_STATE, SSD_GROUP_W), F32),
                     pltpu.VMEM((2 * SSD_HEADS, q), F32),
                     pltpu.VMEM((2 * SSD_HEADS, q), F32)]
    grid_spec = pltpu.PrefetchScalarGridSpec(
        num_scalar_prefetch=1,
        grid=(nchunk,),
        in_specs=chunk_specs(fwd) + chunk_specs(bwd) + [
            pl.BlockSpec((1, 2 * SSD_HEADS), lambda c, f: (0, 0)),
            pl.BlockSpec((1, 2 * SSD_HEADS), lambda c, f: (0, 0)),
            pl.BlockSpec((1, SSD_D_INNER), lambda c, f: (0, 0)),
        ],
        out_specs=[pl.BlockSpec((q, SSD_D_INNER), lambda c, f: (fwd(c), 0)),
                   pl.BlockSpec((q, SSD_D_INNER), lambda c, f: (bwd(c), 0))],
        scratch_shapes=per_direction + per_direction,
    )
    out = jax.ShapeDtypeStruct((t, SSD_D_INNER), BF16)
    return pl.pallas_call(
        _scan_kernel,
        grid_spec=grid_spec,
        out_shape=[out, out],
        compiler_params=_cparams(1),
        name="ssd_scan",
    )(jnp.asarray(flags), xbc, xbc, xbc, dt_raw, xbc, xbc, xbc, dt_raw, dt_bias, a_log, d_row)


def _ssd_out_kernel(yf_ref, yb_ref, z_ref, nw_ref, w_ref, r_ref, o_ref):
    acc = r_ref[...]
    for g in range(SSD_GROUPS):
        cols = slice(g * SSD_GROUP_W, (g + 1) * SSD_GROUP_W)
        y = yf_ref[:, cols].astype(F32) + yb_ref[:, cols].astype(F32)
        gated = y * _silu(z_ref[:, cols].astype(F32))
        yn = _rms(gated, nw_ref[:, cols]).astype(BF16)
        acc = acc + jnp.dot(yn, w_ref[cols, :], preferred_element_type=F32)
    o_ref[...] = acc


def _ssd_out_proj(y_fwd, y_bwd, zx, nw, w, res):
    t, k = y_fwd.shape
    n = w.shape[1]
    bm = _pick(t, 256)
    return pl.pallas_call(
        _ssd_out_kernel,
        grid=(t // bm,),
        in_specs=[
            pl.BlockSpec((bm, k), lambda i: (i, 0)),
            pl.BlockSpec((bm, k), lambda i: (i, 0)),
            pl.BlockSpec((bm, k), lambda i: (i, 0)),
            pl.BlockSpec((1, k), lambda i: (0, 0)),
            pl.BlockSpec((k, n), lambda i: (0, 0), pipeline_mode=pl.Buffered(1)),
            pl.BlockSpec((bm, n), lambda i: (i, 0)),
        ],
        out_specs=pl.BlockSpec((bm, n), lambda i: (i, 0)),
        out_shape=jax.ShapeDtypeStruct((t, n), F32),
        compiler_params=_cparams(1),
        name="ssd_out_proj",
    )(y_fwd, y_bwd, zx, nw, w, res)


def _rope_tables(seqs, t):
    half = DA_HEAD_DIM // 2
    inv = 1.0 / (ROPE_THETA ** (jnp.arange(0, DA_HEAD_DIM, 2, dtype=F32) / DA_HEAD_DIM))
    pos = np.zeros((t,), np.float32)
    for r, s in seqs:
        pos[r:r + s] = np.arange(s, dtype=np.float32)
    ang = jnp.asarray(pos)[:, None] * inv[None, :]
    cos, sin = jnp.cos(ang), jnp.sin(ang)
    cos_t = jnp.concatenate([cos, cos, cos, cos], axis=1)
    sin_t = jnp.concatenate([-sin, -sin, sin, sin], axis=1)
    assert cos_t.shape == (t, LANES) and 4 * half == LANES
    return cos_t, sin_t


def _reorder_qk_columns(w_qkv):
    d = w_qkv.shape[0]
    half = DA_HEAD_DIM // 2
    qk = w_qkv[:, :2 * d].reshape(d, 2 * d // LANES, 2, 2, half)
    qk = qk.transpose(0, 1, 3, 2, 4).reshape(d, 2 * d)
    return jnp.concatenate([qk, w_qkv[:, 2 * d:]], axis=1)


def _seq_groups(seqs):
    groups = []
    for r, s in seqs:
        if groups and groups[-1][2] == s and groups[-1][0] + groups[-1][1] * s == r:
            groups[-1][1] += 1
        else:
            groups.append([r, 1, s])
    return [tuple(g) for g in groups]


def _attention_layer(x_parts, seqs, nw, w_qkv, w_o, lam_p, sub_w, lambda_init):
    t = sum(p.shape[0] for p in x_parts)
    cos_t, sin_t = _rope_tables(seqs, t)
    qkv = _qkv_proj(x_parts, nw, _reorder_qk_columns(w_qkv).astype(BF16), cos_t, sin_t)
    o = None
    for r, n, s in _seq_groups(seqs):
        o = _attention_group(qkv, lam_p, sub_w, r, n, s, lambda_init, o)
    return _matmul_res(o, w_o.astype(BF16), x_parts)


def _ssd_layer(x, seqs, nw, w_in, conv_w, conv_b, dt_bias, a_log, d_skip, norm_w, w_out):
    nzx = SSD_D_INNER + SSD_CONV_DIM
    zx, dt_raw = _ssd_in_proj(x, nw, w_in[:, :nzx].astype(BF16), w_in[:, nzx:].astype(BF16))
    xbc = _conv_silu(zx, conv_w, conv_b.reshape(1, -1), seqs, SSD_D_INNER)
    bias = dt_bias.reshape(1, -1)
    alog = a_log.reshape(1, -1)
    d_row = jnp.repeat(d_skip, SSD_HEAD_DIM).reshape(1, -1)
    y_fwd, y_bwd = _ssd_scan(xbc, dt_raw, bias, alog, d_row, seqs)
    return _ssd_out_proj(y_fwd, y_bwd, zx, norm_w.reshape(1, -1), w_out.astype(BF16), x)


def _trunk(x_parts, seqs, norm_mix, norm_ffn, norm_final, da_w_qkv, da_w_o, da_lambda_q1, da_lambda_k1,
           da_lambda_q2, da_lambda_k2, da_subln, ssd_w_in, ssd_conv_w, ssd_conv_b, ssd_dt_bias,
           ssd_a_log, ssd_d, ssd_norm, ssd_w_out, ffn_w_gu, ffn_w_down):
    depth = norm_mix.shape[0]
    nf = norm_final.reshape(1, -1)
    part_rows = [p.shape[0] for p in x_parts]
    x = x_parts
    for i in range(depth):
        j = i // 2
        nw = norm_mix[i].reshape(1, -1)
        if i % 2 == 0:
            lambda_init = 0.8 - 0.6 * math.exp(-0.3 * i)
            lam_p = jnp.stack([da_lambda_q1[j], da_lambda_k1[j], da_lambda_q2[j], da_lambda_k2[j]])
            y = _attention_layer(x, seqs, nw, da_w_qkv[j], da_w_o[j], lam_p,
                                 da_subln[j].reshape(1, -1), lambda_init)
        else:
            (x0,) = x
            y = _ssd_layer(x0, seqs, nw, ssd_w_in[j], ssd_conv_w[j], ssd_conv_b[j], ssd_dt_bias[j],
                           ssd_a_log[j], ssd_d[j], ssd_norm[j], ssd_w_out[j])
        last = i == depth - 1
        x = tuple(_ffn(y, norm_ffn[i].reshape(1, -1), ffn_w_gu[i].astype(BF16), ffn_w_down[i].astype(BF16),
                       nf, final_norm=last, out_rows=part_rows if last else [y.shape[0]]))
    return x


def kernel(x_prompt, x_sample, norm_mix, norm_ffn, norm_final, da_w_qkv, da_w_o, da_lambda_q1, da_lambda_k1, da_lambda_q2, da_lambda_k2, da_subln, ssd_w_in, ssd_conv_w, ssd_conv_b, ssd_dt_bias, ssd_a_log, ssd_d, ssd_norm, ssd_w_out, ffn_w_gu, ffn_w_down):
    d = x_prompt.shape[-1]
    seqs = []
    for arr in (x_prompt, x_sample):
        for _ in range(arr.shape[0]):
            seqs.append((sum(s for _, s in seqs), arr.shape[1]))
    x_parts = (x_prompt.reshape(-1, d), x_sample.reshape(-1, d))
    y_p, y_s = _trunk(x_parts, tuple(seqs), norm_mix, norm_ffn, norm_final, da_w_qkv, da_w_o,
                      da_lambda_q1, da_lambda_k1, da_lambda_q2, da_lambda_k2, da_subln, ssd_w_in,
                      ssd_conv_w, ssd_conv_b, ssd_dt_bias, ssd_a_log, ssd_d, ssd_norm, ssd_w_out,
                      ffn_w_gu, ffn_w_down)
    return y_p.reshape(x_prompt.shape), y_s.reshape(x_sample.shape)
```

```python
import functools
import math

import numpy as np
import jax
import jax.numpy as jnp
from jax import lax
from jax.experimental import pallas as pl
from jax.experimental.pallas import tpu as pltpu

F32 = jnp.float32
BF16 = jnp.bfloat16

EPS = 1e-5
LOG2E = 1.4426950408889634
LANES = 128
SUBLANES = 8
VMEM_LIMIT = 56 * 1024 * 1024

DA_HEADS = 16
DA_HEAD_DIM = 64
DA_V_DIM = 2 * DA_HEAD_DIM
ROPE_THETA = 10000.0

SSD_HEAD_DIM = 64
SSD_GROUPS = 8
SSD_HPG = 8
SSD_HEADS = SSD_GROUPS * SSD_HPG
SSD_D_STATE = 128
SSD_D_INNER = SSD_HEADS * SSD_HEAD_DIM
SSD_CHUNK = 128
SSD_GROUP_W = SSD_HPG * SSD_HEAD_DIM
SSD_BC_W = SSD_GROUPS * SSD_D_STATE
SSD_CONV_DIM = SSD_D_INNER + 2 * SSD_BC_W


def _cparams(n_axes):
    return pltpu.CompilerParams(dimension_semantics=("arbitrary",) * n_axes,
                                vmem_limit_bytes=VMEM_LIMIT)


def _rms(x, w):
    ms = jnp.mean(x * x, axis=-1, keepdims=True)
    return x * lax.rsqrt(ms + EPS) * w


def _silu(x):
    return x / (1.0 + jnp.exp(-x))


def _pick(n, pref):
    b = min(n, pref)
    while n % b:
        b //= 2
    return b


def _part_offsets(parts, bm):
    offs = [0]
    for p in parts:
        assert p.shape[0] % bm == 0
        offs.append(offs[-1] + p.shape[0] // bm)
    return offs


def _part_spec(block, offs, p, col_fn):
    lo, n = offs[p], offs[p + 1] - offs[p]
    return pl.BlockSpec(block, lambda i, j: (jnp.clip(i - lo, 0, n - 1), col_fn(i, j, lo, n)))


def _in_part(i, offs, p):
    return jnp.logical_and(i >= offs[p], i < offs[p + 1])


def _qkv_kernel(*refs, offs, n_q_tiles, n_rope_tiles):
    nparts = len(offs) - 1
    x_refs = refs[:nparts]
    nw_ref, w_ref, cos_ref, sin_ref, o_ref, h_ref = refs[nparts:]
    i = pl.program_id(0)
    j = pl.program_id(1)

    for p in range(nparts):
        @pl.when(jnp.logical_and(j == 0, _in_part(i, offs, p)))
        def _(p=p):
            h_ref[...] = _rms(x_refs[p][...], nw_ref[...]).astype(BF16)

    y = jnp.dot(h_ref[...], w_ref[...], preferred_element_type=F32)
    rope = j < n_rope_tiles
    c = jnp.where(rope, cos_ref[...], 1.0)
    s = jnp.where(rope, sin_ref[...], 0.0)
    scale = jnp.where(j < n_q_tiles, LOG2E * DA_HEAD_DIM ** -0.5, 1.0)
    c = c * scale
    s = s * scale
    for t in range(y.shape[1] // LANES):
        yt = y[:, t * LANES:(t + 1) * LANES]
        o_ref[:, t * LANES:(t + 1) * LANES] = (yt * c + pltpu.roll(yt, LANES // 2, 1) * s).astype(o_ref.dtype)


def _qkv_proj(x_parts, nw, w, cos_t, sin_t):
    d = x_parts[0].shape[1]
    t = sum(p.shape[0] for p in x_parts)
    n = w.shape[1]
    bm = _pick(math.gcd(*[p.shape[0] for p in x_parts]), 512)
    bn = _pick(d, 2048)
    offs = _part_offsets(x_parts, bm)
    return pl.pallas_call(
        functools.partial(_qkv_kernel, offs=tuple(offs), n_q_tiles=d // bn, n_rope_tiles=2 * d // bn),
        grid=(t // bm, n // bn),
        in_specs=[_part_spec((bm, d), offs, p, lambda i, j, lo, cnt: 0) for p in range(len(x_parts))] + [
            pl.BlockSpec((1, d), lambda i, j: (0, 0)),
            pl.BlockSpec((d, bn), lambda i, j: (0, j)),
            pl.BlockSpec((bm, LANES), lambda i, j: (i, 0)),
            pl.BlockSpec((bm, LANES), lambda i, j: (i, 0)),
        ],
        out_specs=pl.BlockSpec((bm, bn), lambda i, j: (i, j)),
        out_shape=jax.ShapeDtypeStruct((t, n), BF16),
        scratch_shapes=[pltpu.VMEM((bm, d), BF16)],
        compiler_params=_cparams(2),
        name="qkv_proj",
    )(*x_parts, nw, w, cos_t, sin_t)


def _attn_kernel(lam_ref, sub_ref, q_ref, k_ref, v_ref, o_ref, vt_ref, qt_ref, s_ref, smax_ref,
                 m_ref, acc_ref, *, tk, tq, lambda_init):
    n_sub = q_ref.shape[0] // tq
    nk = k_ref.shape[0] // tk

    @pl.when(pl.program_id(2) == 0)
    def _():
        def transpose_chunk(c, carry):
            start = pl.multiple_of(c * tk, tk)
            vt_ref[0:DA_V_DIM, pl.ds(start, tk)] = v_ref[pl.ds(start, tk), :].astype(F32).T.astype(BF16)
            return carry
        lax.fori_loop(0, nk, transpose_chunk, 0)
        extra = lax.broadcasted_iota(jnp.int32, (vt_ref.shape[0] - DA_V_DIM, vt_ref.shape[1]), 0)
        vt_ref[DA_V_DIM:, :] = jnp.where(extra == 0, 1.0, 0.0).astype(BF16)

    row = lax.broadcasted_iota(jnp.int32, (DA_V_DIM, tq), 0)
    in_map0 = (row % DA_HEAD_DIM) < (DA_HEAD_DIM // 2)
    for b in range(n_sub):
        qt = q_ref[b * tq:(b + 1) * tq, :].astype(F32).T
        qt_ref[b, 0] = jnp.where(in_map0, qt, 0.0).astype(BF16)
        qt_ref[b, 1] = jnp.where(in_map0, 0.0, qt).astype(BF16)
    m_ref[...] = jnp.full(m_ref.shape, -jnp.inf, F32)
    acc_ref[...] = jnp.zeros(acc_ref.shape, F32)

    def tile_start(tile):
        return tile * tk if isinstance(tile, int) else pl.multiple_of(tile * tk, tk)

    def scores(b, tile, slot):
        k = k_ref[pl.ds(tile_start(tile), tk), :]
        for mp in range(2):
            st = jnp.dot(k, qt_ref[b, mp], preferred_element_type=F32)
            s_ref[slot, mp] = st
            smax_ref[slot, mp] = jnp.max(st, axis=0, keepdims=True)

    def update(b, tile, slot):
        vt = vt_ref[:, pl.ds(tile_start(tile), tk)]
        for mp in range(2):
            m_old = m_ref[b, mp]
            m_new = jnp.maximum(m_old, smax_ref[slot, mp])
            alpha = jnp.exp2(m_old - m_new)
            p = jnp.exp2(s_ref[slot, mp] - m_new)
            acc_ref[b, mp] = alpha * acc_ref[b, mp] + jnp.dot(vt, p.astype(BF16), preferred_element_type=F32)
            m_ref[b, mp] = m_new

    n_stages = n_sub * nk
    if n_stages > 8:
        def at(u):
            return (u // nk, u % nk) if isinstance(u, int) else (lax.div(u, nk), lax.rem(u, nk))

        scores(0, 0, 0)

        def stage_pair(j):
            u = 2 * j
            scores(*at(u + 1), 1)
            update(*at(u), 0)
            scores(*at(u + 2), 0)
            update(*at(u + 1), 1)

        def body(i, carry):
            stage_pair(2 * i)
            stage_pair(2 * i + 1)
            return carry

        n_pairs = n_stages // 2 - 1
        lax.fori_loop(0, n_pairs // 2, body, 0)
        if n_pairs % 2:
            stage_pair(n_pairs - 1)
        scores(*at(n_stages - 1), 1)
        update(*at(n_stages - 2), 0)
        update(*at(n_stages - 1), 1)
    else:
        stages = [(b, t) for b in range(n_sub) for t in range(nk)]
        scores(*stages[0], 0)
        for i in range(len(stages) - 1):
            scores(*stages[i + 1], (i + 1) % 2)
            update(*stages[i], i % 2)
        update(*stages[-1], (len(stages) - 1) % 2)

    lam_p = lam_ref[...]
    t1 = jnp.sum(lam_p[0:1, :] * lam_p[1:2, :], axis=-1, keepdims=True)
    t2 = jnp.sum(lam_p[2:3, :] * lam_p[3:4, :], axis=-1, keepdims=True)
    lam = jnp.exp(t1) - jnp.exp(t2) + lambda_init
    for b in range(n_sub):
        acc0, acc1 = acc_ref[b, 0], acc_ref[b, 1]
        ot = (acc0[:DA_V_DIM] / acc0[DA_V_DIM:DA_V_DIM + 1]
              - lam * (acc1[:DA_V_DIM] / acc1[DA_V_DIM:DA_V_DIM + 1]))
        ms = jnp.mean(ot * ot, axis=0, keepdims=True)
        o = (ot * lax.rsqrt(ms + EPS)).T
        o_ref[b * tq:(b + 1) * tq, :] = (o * sub_ref[...] * (1.0 - lambda_init)).astype(o_ref.dtype)


def _attn_alias_kernel(lam_ref, sub_ref, q_ref, k_ref, v_ref, prev_ref, o_ref, *scratch, **kw):
    del prev_ref
    _attn_kernel(lam_ref, sub_ref, q_ref, k_ref, v_ref, o_ref, *scratch, **kw)


def _attention_group(qkv, lam_p, sub_w, row0, nseq, s, lambda_init, prev_out):
    tq = _pick(s, 512)
    tk = _pick(s // 2, 1024)
    if (s // tq) * (s // tk) <= 8:
        n_sub = s // tq
    else:
        tq, n_sub = tq // 2, 2
    bq = n_sub * tq
    nq = s // bq
    assert row0 % s == 0 and (s // tk) % 2 == 0
    kw = dict(tk=tk, tq=tq, lambda_init=lambda_init)
    in_specs = [
        pl.BlockSpec((4, DA_HEAD_DIM), lambda b, h, i: (0, 0)),
        pl.BlockSpec((1, DA_V_DIM), lambda b, h, i: (0, 0)),
        pl.BlockSpec((bq, DA_V_DIM), lambda b, h, i: (row0 // bq + b * nq + i, h)),
        pl.BlockSpec((s, DA_V_DIM), lambda b, h, i: (row0 // s + b, DA_HEADS + h)),
        pl.BlockSpec((s, DA_V_DIM), lambda b, h, i: (row0 // s + b, 2 * DA_HEADS + h)),
    ]
    args = [lam_p, sub_w, qkv, qkv, qkv]
    if prev_out is None:
        kern, aliases = functools.partial(_attn_kernel, **kw), {}
    else:
        kern, aliases = functools.partial(_attn_alias_kernel, **kw), {len(args): 0}
        in_specs.append(pl.BlockSpec(memory_space=pl.ANY))
        args.append(prev_out)
    return pl.pallas_call(
        kern,
        grid=(nseq, DA_HEADS, nq),
        in_specs=in_specs,
        out_specs=pl.BlockSpec((bq, DA_V_DIM), lambda b, h, i: (row0 // bq + b * nq + i, h)),
        out_shape=jax.ShapeDtypeStruct((qkv.shape[0], DA_HEADS * DA_V_DIM), BF16),
        input_output_aliases=aliases,
        scratch_shapes=[
            pltpu.VMEM((DA_V_DIM + 16, s), BF16),
            pltpu.VMEM((n_sub, 2, DA_V_DIM, tq), BF16),
            pltpu.VMEM((2, 2, tk, tq), F32),
            pltpu.VMEM((2, 2, 1, tq), F32),
            pltpu.VMEM((n_sub, 2, 1, tq), F32),
            pltpu.VMEM((n_sub, 2, DA_V_DIM + 16, tq), F32),
        ],
        compiler_params=_cparams(3),
        name="diff_attention",
    )(*args)


def _matmul_res_kernel(a_ref, w_ref, *refs, offs):
    r_refs, o_ref = refs[:-1], refs[-1]
    i = pl.program_id(0)
    y = jnp.dot(a_ref[...], w_ref[...], preferred_element_type=F32)
    for p, r_ref in enumerate(r_refs):
        @pl.when(_in_part(i, offs, p))
        def _(r_ref=r_ref):
            o_ref[...] = r_ref[...] + y


def _matmul_res(a, w, res_parts):
    t, k = a.shape
    n = w.shape[1]
    bm = _pick(math.gcd(*[p.shape[0] for p in res_parts]), 1024)
    bn = _pick(n, 1024)
    offs = _part_offsets(res_parts, bm)
    col = lambda i, j, lo, cnt: jnp.where(jnp.logical_and(i >= lo, i < lo + cnt), j, 0)
    return pl.pallas_call(
        functools.partial(_matmul_res_kernel, offs=tuple(offs)),
        grid=(t // bm, n // bn),
        in_specs=[
            pl.BlockSpec((bm, k), lambda i, j: (i, 0)),
            pl.BlockSpec((k, bn), lambda i, j: (0, j)),
        ] + [_part_spec((bm, bn), offs, p, col) for p in range(len(res_parts))],
        out_specs=pl.BlockSpec((bm, bn), lambda i, j: (i, j)),
        out_shape=jax.ShapeDtypeStruct((t, n), F32),
        compiler_params=_cparams(2),
        name="matmul_residual",
    )(a, w, *res_parts)


def _ffn_kernel(x_ref, nw_ref, wg_ref, wu_ref, wd_ref, nf_ref, *refs, offs, final_norm):
    o_refs, (h_ref, acc_ref) = refs[:-2], refs[-2:]
    i = pl.program_id(0)
    j = pl.program_id(1)

    @pl.when(j == 0)
    def _():
        h_ref[...] = _rms(x_ref[...], nw_ref[...]).astype(BF16)
        acc_ref[...] = jnp.zeros_like(acc_ref)

    h = h_ref[...]
    g = jnp.dot(h, wg_ref[...], preferred_element_type=F32)
    u = jnp.dot(h, wu_ref[...], preferred_element_type=F32)
    a = (_silu(g) * u).astype(BF16)
    acc_ref[...] += jnp.dot(a, wd_ref[...], preferred_element_type=F32)

    last = j == pl.num_programs(1) - 1
    for p, o_ref in enumerate(o_refs):
        @pl.when(jnp.logical_and(last, _in_part(i, offs, p)))
        def _(o_ref=o_ref):
            r = acc_ref[...] + x_ref[...]
            if final_norm:
                r = _rms(r, nf_ref[...])
            o_ref[...] = r


def _ffn(x, nw, w_gu, w_down, nf, final_norm, out_rows):
    t, d = x.shape
    dff = w_down.shape[0]
    bm = _pick(math.gcd(*out_rows), 512)
    bf = _pick(dff, 512)
    nf_tiles = dff // bf
    outs = [jax.ShapeDtypeStruct((r, d), F32) for r in out_rows]
    offs = _part_offsets(outs, bm)
    return pl.pallas_call(
        functools.partial(_ffn_kernel, offs=tuple(offs), final_norm=final_norm),
        grid=(t // bm, nf_tiles),
        in_specs=[
            pl.BlockSpec((bm, d), lambda i, j: (i, 0)),
            pl.BlockSpec((1, d), lambda i, j: (0, 0)),
            pl.BlockSpec((d, bf), lambda i, j: (0, j)),
            pl.BlockSpec((d, bf), lambda i, j: (0, j + nf_tiles)),
            pl.BlockSpec((bf, d), lambda i, j: (j, 0)),
            pl.BlockSpec((1, d), lambda i, j: (0, 0)),
        ],
        out_specs=[_part_spec((bm, d), offs, p, lambda i, j, lo, cnt: 0) for p in range(len(outs))],
        out_shape=outs,
        scratch_shapes=[pltpu.VMEM((bm, d), BF16), pltpu.VMEM((bm, d), F32)],
        compiler_params=_cparams(2),
        name="swiglu_ffn",
    )(x, nw, w_gu, w_gu, w_down, nf)


def _ssd_in_kernel(x_ref, nw_ref, w_ref, wdt_ref, o_ref, dt_ref, h_ref):
    j = pl.program_id(1)

    @pl.when(j == 0)
    def _():
        h = _rms(x_ref[...], nw_ref[...]).astype(BF16)
        h_ref[...] = h
        dt_ref[...] = jnp.dot(h, wdt_ref[...], preferred_element_type=F32)

    o_ref[...] = jnp.dot(h_ref[...], w_ref[...], preferred_element_type=F32).astype(o_ref.dtype)


def _ssd_in_proj(x, nw, w_zx, w_dt):
    t, d = x.shape
    n = w_zx.shape[1]
    ndt = w_dt.shape[1]
    bm = _pick(t, 1024)
    bn = _pick(n, 1024)
    return pl.pallas_call(
        _ssd_in_kernel,
        grid=(t // bm, n // bn),
        in_specs=[
            pl.BlockSpec((bm, d), lambda i, j: (i, 0)),
            pl.BlockSpec((1, d), lambda i, j: (0, 0)),
            pl.BlockSpec((d, bn), lambda i, j: (0, j)),
            pl.BlockSpec((d, ndt), lambda i, j: (0, 0)),
        ],
        out_specs=[
            pl.BlockSpec((bm, bn), lambda i, j: (i, j)),
            pl.BlockSpec((bm, ndt), lambda i, j: (i, 0)),
        ],
        out_shape=[jax.ShapeDtypeStruct((t, n), BF16), jax.ShapeDtypeStruct((t, ndt), F32)],
        scratch_shapes=[pltpu.VMEM((bm, d), BF16)],
        compiler_params=_cparams(2),
        name="ssd_in_proj",
    )(x, nw, w_zx, w_dt)


def _conv_kernel(flags_ref, prev_ref, cur_ref, next_ref, w_ref, b_ref, o_ref):
    f = flags_ref[pl.program_id(0)]
    at_start = (f & 1) == 1
    at_end = (f & 2) == 2
    x = cur_ref[...].astype(F32)
    bt = x.shape[0]
    row = lax.broadcasted_iota(jnp.int32, x.shape, 0)
    before = jnp.where(at_start, 0.0, prev_ref[SUBLANES - 1:SUBLANES, :].astype(F32))
    after = jnp.where(at_end, 0.0, next_ref[0:2, :].astype(F32))
    xm1 = jnp.where(row == 0, before, pltpu.roll(x, 1, 0))
    xp1 = jnp.where(row == bt - 1, after[0:1], pltpu.roll(x, bt - 1, 0))
    xp2 = jnp.where(row == bt - 2, after[0:1],
                    jnp.where(row == bt - 1, after[1:2], pltpu.roll(x, bt - 2, 0)))
    w = w_ref[...]
    y = xm1 * w[0:1] + x * w[1:2] + xp1 * w[2:3] + xp2 * w[3:4] + b_ref[...]
    o_ref[...] = _silu(y).astype(o_ref.dtype)


def _conv_silu(zx, conv_w, conv_b, seqs, col0):
    t = zx.shape[0]
    ncol = conv_w.shape[1]
    bt = _pick(math.gcd(*[s for _, s in seqs]), 512)
    bc = _pick(ncol, 1024)
    nblk = t // bt
    starts = {r for r, _ in seqs}
    ends = {r + s for r, s in seqs}
    flags = np.array([(1 if i * bt in starts else 0) | (2 if (i + 1) * bt in ends else 0)
                      for i in range(nblk)], np.int32)
    hb = bt // SUBLANES
    c0 = col0 // bc
    grid_spec = pltpu.PrefetchScalarGridSpec(
        num_scalar_prefetch=1,
        grid=(nblk, ncol // bc),
        in_specs=[
            pl.BlockSpec((SUBLANES, bc), lambda i, j, f: (jnp.maximum(i * hb - 1, 0), c0 + j)),
            pl.BlockSpec((bt, bc), lambda i, j, f: (i, c0 + j)),
            pl.BlockSpec((SUBLANES, bc), lambda i, j, f: (jnp.minimum((i + 1) * hb, t // SUBLANES - 1), c0 + j)),
            pl.BlockSpec((4, bc), lambda i, j, f: (0, j)),
            pl.BlockSpec((1, bc), lambda i, j, f: (0, j)),
        ],
        out_specs=pl.BlockSpec((bt, bc), lambda i, j, f: (i, j)),
    )
    return pl.pallas_call(
        _conv_kernel,
        grid_spec=grid_spec,
        out_shape=jax.ShapeDtypeStruct((t, ncol), BF16),
        compiler_params=_cparams(2),
        name="ssd_conv_silu",
    )(jnp.asarray(flags), zx, zx, zx, conv_w, conv_b)


def _scan_kernel(flags_ref, xs_f, b_f, c_f, dt_f, xs_b, b_b, c_b, dt_b, bias_ref, alog_ref, d_ref,
                 yf_ref, yb_ref, st_f, ut_f, wt_f, st_b, ut_b, wt_b):
    f = flags_ref[pl.program_id(0)]

    @pl.when((f & 1) == 1)
    def _():
        st_f[...] = jnp.zeros_like(st_f)

    @pl.when((f & 2) == 2)
    def _():
        st_b[...] = jnp.zeros_like(st_b)

    _scan_direction(xs_f, b_f, c_f, dt_f, bias_ref, alog_ref, d_ref, yf_ref, st_f, ut_f, wt_f, reverse=False)
    _scan_direction(xs_b, b_b, c_b, dt_b, bias_ref, alog_ref, d_ref, yb_ref, st_b, ut_b, wt_b, reverse=True)


def _scan_direction(xs_ref, b_ref, c_ref, dt_ref, bias_ref, alog_ref, d_ref, y_ref, state_ref, ut_ref,
                    wt_ref, *, reverse):
    q = SSD_CHUNK
    dir_off = SSD_HEADS if reverse else 0
    xdt_in = dt_ref[...] + bias_ref[...]
    dt = jnp.maximum(xdt_in, 0.0) + jnp.log(1.0 + jnp.exp(-jnp.abs(xdt_in)))
    da = dt * (-jnp.exp(alog_ref[...]))
    row = lax.broadcasted_iota(jnp.int32, (q, LANES), 0)
    cum = da
    k = 1
    while k < q:
        cum = cum + jnp.where(row >= k, pltpu.roll(cum, k, 0), 0.0)
        k *= 2
    tot = cum[q - 1:q, :]
    u = (tot + da - cum) if reverse else cum
    state_decay = jnp.exp(tot)
    u2 = u * LOG2E
    ut_ref[...] = (u2 - jnp.log2(dt)).T
    wt_ref[...] = (dt * jnp.exp(tot - u)).T

    li = lax.broadcasted_iota(jnp.int32, (q, q), 0)
    si = lax.broadcasted_iota(jnp.int32, (q, q), 1)
    allowed = (si >= li) if reverse else (li >= si)
    low = lax.broadcasted_iota(jnp.int32, (q, LANES), 1) < SSD_HEAD_DIM

    for g in range(SSD_GROUPS):
        bg = b_ref[:, g * SSD_D_STATE:(g + 1) * SSD_D_STATE]
        cg = c_ref[:, g * SSD_D_STATE:(g + 1) * SSD_D_STATE]
        bt = bg.astype(F32).T
        cb = jnp.dot(cg, bt.astype(BF16), preferred_element_type=F32)
        st = state_ref[g]
        y_off = jnp.dot(cg, st.astype(BF16), preferred_element_type=F32)
        for j in range(SSD_HPG // 2):
            cols = slice(g * SSD_GROUP_W + j * LANES, g * SSD_GROUP_W + (j + 1) * LANES)
            tile = slice(j * LANES, (j + 1) * LANES)
            att, inj, scale = [], [], []
            for r in (2 * j, 2 * j + 1):
                h = dir_off + g * SSD_HPG + r
                ucol = jnp.broadcast_to(u2[:, h:h + 1], (q, q))
                decay_dt = jnp.exp2(jnp.where(allowed, ucol - ut_ref[h:h + 1, :], -jnp.inf))
                att.append((cb * decay_dt).astype(BF16))
                inj.append((bt * wt_ref[h:h + 1, :]).astype(BF16))
                scale.append(jnp.exp2(ucol))
            xt = xs_ref[:, cols]
            zt = jnp.zeros_like(xt)
            rhs = jnp.concatenate([jnp.where(low, xt, zt), jnp.where(low, zt, xt)], axis=0)
            lhs = jnp.concatenate([jnp.concatenate(att, axis=1), jnp.concatenate(inj, axis=1)], axis=0)
            res = jnp.dot(lhs, rhs, preferred_element_type=F32)
            y = res[:q] + y_off[:, tile] * jnp.where(low, scale[0], scale[1])
            if not reverse:
                y = y + d_ref[:, cols] * xt.astype(F32)
            y_ref[:, cols] = y.astype(y_ref.dtype)
            h = dir_off + g * SSD_HPG + 2 * j
            keep = jnp.where(low[0:1], state_decay[:, h:h + 1], state_decay[:, h + 1:h + 2])
            state_ref[g, :, tile] = st[:, tile] * keep + res[q:]


def _ssd_scan(xbc, dt_raw, dt_bias, a_log, d_row, seqs):
    t = xbc.shape[0]
    q = SSD_CHUNK
    nchunk = t // q
    starts = {r // q for r, _ in seqs}
    ends = {(r + s) // q - 1 for r, s in seqs}
    flags = np.array([(1 if c in starts else 0) | (2 if nchunk - 1 - c in ends else 0)
                      for c in range(nchunk)], np.int32)
    fwd = lambda c: c
    bwd = lambda c: nchunk - 1 - c
    nxb = SSD_D_INNER // SSD_BC_W

    def chunk_specs(blk):
        return [
            pl.BlockSpec((q, SSD_D_INNER), lambda c, f: (blk(c), 0)),
            pl.BlockSpec((q, SSD_BC_W), lambda c, f: (blk(c), nxb)),
            pl.BlockSpec((q, SSD_BC_W), lambda c, f: (blk(c), nxb + 1)),
            pl.BlockSpec((q, 2 * SSD_HEADS), lambda c, f: (blk(c), 0)),
        ]

    per_direction = [pltpu.VMEM((SSD_GROUPS, SSD_D_STATE, SSD_GROUP_W), F32),
                     pltpu.VMEM((2 * SSD_HEADS, q), F32),
                     pltpu.VMEM((2 * SSD_HEADS, q), F32)]
    grid_spec = pltpu.PrefetchScalarGridSpec(
        num_scalar_prefetch=1,
        grid=(nchunk,),
        in_specs=chunk_specs(fwd) + chunk_specs(bwd) + [
            pl.BlockSpec((1, 2 * SSD_HEADS), lambda c, f: (0, 0)),
            pl.BlockSpec((1, 2 * SSD_HEADS), lambda c, f: (0, 0)),
            pl.BlockSpec((1, SSD_D_INNER), lambda c, f: (0, 0)),
        ],
        out_specs=[pl.BlockSpec((q, SSD_D_INNER), lambda c, f: (fwd(c), 0)),
                   pl.BlockSpec((q, SSD_D_INNER), lambda c, f: (bwd(c), 0))],
        scratch_shapes=per_direction + per_direction,
    )
    out = jax.ShapeDtypeStruct((t, SSD_D_INNER), BF16)
    return pl.pallas_call(
        _scan_kernel,
        grid_spec=grid_spec,
        out_shape=[out, out],
        compiler_params=_cparams(1),
        name="ssd_scan",
    )(jnp.asarray(flags), xbc, xbc, xbc, dt_raw, xbc, xbc, xbc, dt_raw, dt_bias, a_log, d_row)


def _ssd_out_kernel(yf_ref, yb_ref, z_ref, nw_ref, w_ref, r_ref, o_ref):
    acc = r_ref[...]
    for g in range(SSD_GROUPS):
        cols = slice(g * SSD_GROUP_W, (g + 1) * SSD_GROUP_W)
        y = yf_ref[:, cols].astype(F32) + yb_ref[:, cols].astype(F32)
        gated = y * _silu(z_ref[:, cols].astype(F32))
        yn = _rms(gated, nw_ref[:, cols]).astype(BF16)
        acc = acc + jnp.dot(yn, w_ref[cols, :], preferred_element_type=F32)
    o_ref[...] = acc


def _ssd_out_proj(y_fwd, y_bwd, zx, nw, w, res):
    t, k = y_fwd.shape
    n = w.shape[1]
    bm = _pick(t, 256)
    return pl.pallas_call(
        _ssd_out_kernel,
        grid=(t // bm,),
        in_specs=[
            pl.BlockSpec((bm, k), lambda i: (i, 0)),
            pl.BlockSpec((bm, k), lambda i: (i, 0)),
            pl.BlockSpec((bm, k), lambda i: (i, 0)),
            pl.BlockSpec((1, k), lambda i: (0, 0)),
            pl.BlockSpec((k, n), lambda i: (0, 0), pipeline_mode=pl.Buffered(1)),
            pl.BlockSpec((bm, n), lambda i: (i, 0)),
        ],
        out_specs=pl.BlockSpec((bm, n), lambda i: (i, 0)),
        out_shape=jax.ShapeDtypeStruct((t, n), F32),
        compiler_params=_cparams(1),
        name="ssd_out_proj",
    )(y_fwd, y_bwd, zx, nw, w, res)


def _rope_tables(seqs, t):
    half = DA_HEAD_DIM // 2
    inv = 1.0 / (ROPE_THETA ** (jnp.arange(0, DA_HEAD_DIM, 2, dtype=F32) / DA_HEAD_DIM))
    pos = np.zeros((t,), np.float32)
    for r, s in seqs:
        pos[r:r + s] = np.arange(s, dtype=np.float32)
    ang = jnp.asarray(pos)[:, None] * inv[None, :]
    cos, sin = jnp.cos(ang), jnp.sin(ang)
    cos_t = jnp.concatenate([cos, cos, cos, cos], axis=1)
    sin_t = jnp.concatenate([-sin, -sin, sin, sin], axis=1)
    assert cos_t.shape == (t, LANES) and 4 * half == LANES
    return cos_t, sin_t


def _reorder_qk_columns(w_qkv):
    d = w_qkv.shape[0]
    half = DA_HEAD_DIM // 2
    qk = w_qkv[:, :2 * d].reshape(d, 2 * d // LANES, 2, 2, half)
    qk = qk.transpose(0, 1, 3, 2, 4).reshape(d, 2 * d)
    return jnp.concatenate([qk, w_qkv[:, 2 * d:]], axis=1)


def _seq_groups(seqs):
    groups = []
    for r, s in seqs:
        if groups and groups[-1][2] == s and groups[-1][0] + groups[-1][1] * s == r:
            groups[-1][1] += 1
        else:
            groups.append([r, 1, s])
    return [tuple(g) for g in groups]


def _attention_layer(x_parts, seqs, nw, w_qkv, w_o, lam_p, sub_w, lambda_init):
    t = sum(p.shape[0] for p in x_parts)
    cos_t, sin_t = _rope_tables(seqs, t)
    qkv = _qkv_proj(x_parts, nw, _reorder_qk_columns(w_qkv).astype(BF16), cos_t, sin_t)
    o = None
    for r, n, s in _seq_groups(seqs):
        o = _attention_group(qkv, lam_p, sub_w, r, n, s, lambda_init, o)
    return _matmul_res(o, w_o.astype(BF16), x_parts)


def _ssd_layer(x, seqs, nw, w_in, conv_w, conv_b, dt_bias, a_log, d_skip, norm_w, w_out):
    nzx = SSD_D_INNER + SSD_CONV_DIM
    zx, dt_raw = _ssd_in_proj(x, nw, w_in[:, :nzx].astype(BF16), w_in[:, nzx:].astype(BF16))
    xbc = _conv_silu(zx, conv_w, conv_b.reshape(1, -1), seqs, SSD_D_INNER)
    bias = dt_bias.reshape(1, -1)
    alog = a_log.reshape(1, -1)
    d_row = jnp.repeat(d_skip, SSD_HEAD_DIM).reshape(1, -1)
    y_fwd, y_bwd = _ssd_scan(xbc, dt_raw, bias, alog, d_row, seqs)
    return _ssd_out_proj(y_fwd, y_bwd, zx, norm_w.reshape(1, -1), w_out.astype(BF16), x)


def _trunk(x_parts, seqs, norm_mix, norm_ffn, norm_final, da_w_qkv, da_w_o, da_lambda_q1, da_lambda_k1,
           da_lambda_q2, da_lambda_k2, da_subln, ssd_w_in, ssd_conv_w, ssd_conv_b, ssd_dt_bias,
           ssd_a_log, ssd_d, ssd_norm, ssd_w_out, ffn_w_gu, ffn_w_down):
    depth = norm_mix.shape[0]
    nf = norm_final.reshape(1, -1)
    part_rows = [p.shape[0] for p in x_parts]
    x = x_parts
    for i in range(depth):
        j = i // 2
        nw = norm_mix[i].reshape(1, -1)
        if i % 2 == 0:
            lambda_init = 0.8 - 0.6 * math.exp(-0.3 * i)
            lam_p = jnp.stack([da_lambda_q1[j], da_lambda_k1[j], da_lambda_q2[j], da_lambda_k2[j]])
            y = _attention_layer(x, seqs, nw, da_w_qkv[j], da_w_o[j], lam_p,
                                 da_subln[j].reshape(1, -1), lambda_init)
        else:
            (x0,) = x
            y = _ssd_layer(x0, seqs, nw, ssd_w_in[j], ssd_conv_w[j], ssd_conv_b[j], ssd_dt_bias[j],
                           ssd_a_log[j], ssd_d[j], ssd_norm[j], ssd_w_out[j])
        last = i == depth - 1
        x = tuple(_ffn(y, norm_ffn[i].reshape(1, -1), ffn_w_gu[i].astype(BF16), ffn_w_down[i].astype(BF16),
                       nf, final_norm=last, out_rows=part_rows if last else [y.shape[0]]))
    return x


def kernel(x_prompt, x_sample, norm_mix, norm_ffn, norm_final, da_w_qkv, da_w_o, da_lambda_q1, da_lambda_k1, da_lambda_q2, da_lambda_k2, da_subln, ssd_w_in, ssd_conv_w, ssd_conv_b, ssd_dt_bias, ssd_a_log, ssd_d, ssd_norm, ssd_w_out, ffn_w_gu, ffn_w_down):
    d = x_prompt.shape[-1]
    seqs = []
    for arr in (x_prompt, x_sample):
        for _ in range(arr.shape[0]):
            seqs.append((sum(s for _, s in seqs), arr.shape[1]))
    x_parts = (x_prompt.reshape(-1, d), x_sample.reshape(-1, d))
    y_p, y_s = _trunk(x_parts, tuple(seqs), norm_mix, norm_ffn, norm_final, da_w_qkv, da_w_o,
                      da_lambda_q1, da_lambda_k1, da_lambda_q2, da_lambda_k2, da_subln, ssd_w_in,
                      ssd_conv_w, ssd_conv_b, ssd_dt_bias, ssd_a_log, ssd_d, ssd_norm, ssd_w_out,
                      ffn_w_gu, ffn_w_down)
    return y_p.reshape(x_prompt.shape), y_s.reshape(x_sample.shape)
```

```python
import functools
import math

import numpy as np
import jax
import jax.numpy as jnp
from jax import lax
from jax.experimental import pallas as pl
from jax.experimental.pallas import tpu as pltpu

F32 = jnp.float32
BF16 = jnp.bfloat16

EPS = 1e-5
LOG2E = 1.4426950408889634
LANES = 128
SUBLANES = 8
BF16_ROWS = 2 * SUBLANES
VMEM_LIMIT = 56 * 1024 * 1024

QKV_TILE = (512, 2048)
RES_TILE = (1024, 1024)
FFN_TILE = (512, 512)
SSD_IN_TILE = (1024, 1024)
CONV_TILE = (512, 1024)
SSD_OUT_ROWS = 256
ATTN_TQ = 512
ATTN_TK = 1024
ATTN_MAX_UNROLLED_STAGES = 8

DA_HEADS = 16
DA_HEAD_DIM = 64
DA_V_DIM = 2 * DA_HEAD_DIM
ROPE_THETA = 10000.0

SSD_HEAD_DIM = 64
SSD_GROUPS = 8
SSD_HPG = 8
SSD_HEADS = SSD_GROUPS * SSD_HPG
SSD_D_STATE = 128
SSD_D_INNER = SSD_HEADS * SSD_HEAD_DIM
SSD_CHUNK = 128
SSD_GROUP_W = SSD_HPG * SSD_HEAD_DIM
SSD_BC_W = SSD_GROUPS * SSD_D_STATE
SSD_CONV_DIM = SSD_D_INNER + 2 * SSD_BC_W


def _cparams(n_axes):
    return pltpu.CompilerParams(dimension_semantics=("arbitrary",) * n_axes,
                                vmem_limit_bytes=VMEM_LIMIT)


def _rms(x, w):
    ms = jnp.mean(x * x, axis=-1, keepdims=True)
    return x * lax.rsqrt(ms + EPS) * w


def _silu(x):
    return x / (1.0 + jnp.exp(-x))


def _pick(n, pref):
    b = min(n, pref)
    while n % b:
        b //= 2
    return b


def _part_offsets(parts, bm):
    offs = [0]
    for p in parts:
        assert p.shape[0] % bm == 0
        offs.append(offs[-1] + p.shape[0] // bm)
    return offs


def _part_spec(block, offs, p, col_fn):
    lo, n = offs[p], offs[p + 1] - offs[p]
    return pl.BlockSpec(block, lambda i, j: (jnp.clip(i - lo, 0, n - 1), col_fn(i, j, lo, n)))


def _in_part(i, offs, p):
    return jnp.logical_and(i >= offs[p], i < offs[p + 1])


def _qkv_kernel(*refs, offs, n_q_tiles, n_rope_tiles):
    nparts = len(offs) - 1
    x_refs = refs[:nparts]
    nw_ref, w_ref, cos_ref, sin_ref, o_ref, h_ref = refs[nparts:]
    i = pl.program_id(0)
    j = pl.program_id(1)

    for p in range(nparts):
        @pl.when(jnp.logical_and(j == 0, _in_part(i, offs, p)))
        def _(p=p):
            h_ref[...] = _rms(x_refs[p][...], nw_ref[...]).astype(BF16)

    y = jnp.dot(h_ref[...], w_ref[...], preferred_element_type=F32)
    rope = j < n_rope_tiles
    c = jnp.where(rope, cos_ref[...], 1.0)
    s = jnp.where(rope, sin_ref[...], 0.0)
    scale = jnp.where(j < n_q_tiles, LOG2E * DA_HEAD_DIM ** -0.5, 1.0)
    c = c * scale
    s = s * scale
    for t in range(y.shape[1] // LANES):
        yt = y[:, t * LANES:(t + 1) * LANES]
        o_ref[:, t * LANES:(t + 1) * LANES] = (yt * c + pltpu.roll(yt, LANES // 2, 1) * s).astype(o_ref.dtype)


def _qkv_proj(x_parts, nw, w, cos_t, sin_t):
    d = x_parts[0].shape[1]
    t = sum(p.shape[0] for p in x_parts)
    n = w.shape[1]
    bm = _pick(math.gcd(*[p.shape[0] for p in x_parts]), QKV_TILE[0])
    bn = _pick(d, QKV_TILE[1])
    offs = _part_offsets(x_parts, bm)
    return pl.pallas_call(
        functools.partial(_qkv_kernel, offs=tuple(offs), n_q_tiles=d // bn, n_rope_tiles=2 * d // bn),
        grid=(t // bm, n // bn),
        in_specs=[_part_spec((bm, d), offs, p, lambda i, j, lo, cnt: 0) for p in range(len(x_parts))] + [
            pl.BlockSpec((1, d), lambda i, j: (0, 0)),
            pl.BlockSpec((d, bn), lambda i, j: (0, j)),
            pl.BlockSpec((bm, LANES), lambda i, j: (i, 0)),
            pl.BlockSpec((bm, LANES), lambda i, j: (i, 0)),
        ],
        out_specs=pl.BlockSpec((bm, bn), lambda i, j: (i, j)),
        out_shape=jax.ShapeDtypeStruct((t, n), BF16),
        scratch_shapes=[pltpu.VMEM((bm, d), BF16)],
        compiler_params=_cparams(2),
        name="qkv_proj",
    )(*x_parts, nw, w, cos_t, sin_t)


def _attn_kernel(lam_ref, sub_ref, q_ref, k_ref, v_ref, o_ref, vt_ref, qt_ref, s_ref, smax_ref,
                 m_ref, acc_ref, *, tk, tq, lambda_init):
    n_sub = q_ref.shape[0] // tq
    nk = k_ref.shape[0] // tk

    @pl.when(pl.program_id(2) == 0)
    def _():
        def transpose_chunk(c, carry):
            start = pl.multiple_of(c * tk, tk)
            vt_ref[0:DA_V_DIM, pl.ds(start, tk)] = v_ref[pl.ds(start, tk), :].astype(F32).T.astype(BF16)
            return carry
        lax.fori_loop(0, nk, transpose_chunk, 0)
        extra = lax.broadcasted_iota(jnp.int32, (vt_ref.shape[0] - DA_V_DIM, vt_ref.shape[1]), 0)
        vt_ref[DA_V_DIM:, :] = jnp.where(extra == 0, 1.0, 0.0).astype(BF16)

    row = lax.broadcasted_iota(jnp.int32, (DA_V_DIM, tq), 0)
    in_map0 = (row % DA_HEAD_DIM) < (DA_HEAD_DIM // 2)
    for b in range(n_sub):
        qt = q_ref[b * tq:(b + 1) * tq, :].astype(F32).T
        qt_ref[b, 0] = jnp.where(in_map0, qt, 0.0).astype(BF16)
        qt_ref[b, 1] = jnp.where(in_map0, 0.0, qt).astype(BF16)
    m_ref[...] = jnp.full(m_ref.shape, -jnp.inf, F32)
    acc_ref[...] = jnp.zeros(acc_ref.shape, F32)

    def tile_start(tile):
        return tile * tk if isinstance(tile, int) else pl.multiple_of(tile * tk, tk)

    def scores(b, tile, slot):
        k = k_ref[pl.ds(tile_start(tile), tk), :]
        for mp in range(2):
            st = jnp.dot(k, qt_ref[b, mp], preferred_element_type=F32)
            s_ref[slot, mp] = st
            smax_ref[slot, mp] = jnp.max(st, axis=0, keepdims=True)

    def update(b, tile, slot):
        vt = vt_ref[:, pl.ds(tile_start(tile), tk)]
        for mp in range(2):
            m_old = m_ref[b, mp]
            m_new = jnp.maximum(m_old, smax_ref[slot, mp])
            alpha = jnp.exp2(m_old - m_new)
            p = jnp.exp2(s_ref[slot, mp] - m_new)
            acc_ref[b, mp] = alpha * acc_ref[b, mp] + jnp.dot(vt, p.astype(BF16), preferred_element_type=F32)
            m_ref[b, mp] = m_new

    n_stages = n_sub * nk
    if n_stages > ATTN_MAX_UNROLLED_STAGES:
        def at(u):
            return (u // nk, u % nk) if isinstance(u, int) else (lax.div(u, nk), lax.rem(u, nk))

        scores(0, 0, 0)

        def stage_pair(j):
            u = 2 * j
            scores(*at(u + 1), 1)
            update(*at(u), 0)
            scores(*at(u + 2), 0)
            update(*at(u + 1), 1)

        def body(i, carry):
            stage_pair(2 * i)
            stage_pair(2 * i + 1)
            return carry

        n_pairs = n_stages // 2 - 1
        lax.fori_loop(0, n_pairs // 2, body, 0)
        if n_pairs % 2:
            stage_pair(n_pairs - 1)
        scores(*at(n_stages - 1), 1)
        update(*at(n_stages - 2), 0)
        update(*at(n_stages - 1), 1)
    else:
        stages = [(b, t) for b in range(n_sub) for t in range(nk)]
        scores(*stages[0], 0)
        for i in range(len(stages) - 1):
            scores(*stages[i + 1], (i + 1) % 2)
            update(*stages[i], i % 2)
        update(*stages[-1], (len(stages) - 1) % 2)

    lam_p = lam_ref[...]
    t1 = jnp.sum(lam_p[0:1, :] * lam_p[1:2, :], axis=-1, keepdims=True)
    t2 = jnp.sum(lam_p[2:3, :] * lam_p[3:4, :], axis=-1, keepdims=True)
    lam = jnp.exp(t1) - jnp.exp(t2) + lambda_init
    for b in range(n_sub):
        acc0, acc1 = acc_ref[b, 0], acc_ref[b, 1]
        ot = (acc0[:DA_V_DIM] / acc0[DA_V_DIM:DA_V_DIM + 1]
              - lam * (acc1[:DA_V_DIM] / acc1[DA_V_DIM:DA_V_DIM + 1]))
        ms = jnp.mean(ot * ot, axis=0, keepdims=True)
        o = (ot * lax.rsqrt(ms + EPS)).T
        o_ref[b * tq:(b + 1) * tq, :] = (o * sub_ref[...] * (1.0 - lambda_init)).astype(o_ref.dtype)


def _attn_alias_kernel(lam_ref, sub_ref, q_ref, k_ref, v_ref, prev_ref, o_ref, *scratch, **kw):
    del prev_ref
    _attn_kernel(lam_ref, sub_ref, q_ref, k_ref, v_ref, o_ref, *scratch, **kw)


def _attention_group(qkv, lam_p, sub_w, row0, nseq, s, lambda_init, prev_out):
    tq = _pick(s, ATTN_TQ)
    tk = _pick(s // 2, ATTN_TK)
    if (s // tq) * (s // tk) <= ATTN_MAX_UNROLLED_STAGES:
        n_sub = s // tq
    else:
        tq, n_sub = tq // 2, 2
    bq = n_sub * tq
    nq = s // bq
    assert row0 % s == 0 and (s // tk) % 2 == 0
    kw = dict(tk=tk, tq=tq, lambda_init=lambda_init)
    in_specs = [
        pl.BlockSpec((4, DA_HEAD_DIM), lambda b, h, i: (0, 0)),
        pl.BlockSpec((1, DA_V_DIM), lambda b, h, i: (0, 0)),
        pl.BlockSpec((bq, DA_V_DIM), lambda b, h, i: (row0 // bq + b * nq + i, h)),
        pl.BlockSpec((s, DA_V_DIM), lambda b, h, i: (row0 // s + b, DA_HEADS + h)),
        pl.BlockSpec((s, DA_V_DIM), lambda b, h, i: (row0 // s + b, 2 * DA_HEADS + h)),
    ]
    args = [lam_p, sub_w, qkv, qkv, qkv]
    if prev_out is None:
        kern, aliases = functools.partial(_attn_kernel, **kw), {}
    else:
        kern, aliases = functools.partial(_attn_alias_kernel, **kw), {len(args): 0}
        in_specs.append(pl.BlockSpec(memory_space=pl.ANY))
        args.append(prev_out)
    return pl.pallas_call(
        kern,
        grid=(nseq, DA_HEADS, nq),
        in_specs=in_specs,
        out_specs=pl.BlockSpec((bq, DA_V_DIM), lambda b, h, i: (row0 // bq + b * nq + i, h)),
        out_shape=jax.ShapeDtypeStruct((qkv.shape[0], DA_HEADS * DA_V_DIM), BF16),
        input_output_aliases=aliases,
        scratch_shapes=[
            pltpu.VMEM((DA_V_DIM + BF16_ROWS, s), BF16),
            pltpu.VMEM((n_sub, 2, DA_V_DIM, tq), BF16),
            pltpu.VMEM((2, 2, tk, tq), F32),
            pltpu.VMEM((2, 2, 1, tq), F32),
            pltpu.VMEM((n_sub, 2, 1, tq), F32),
            pltpu.VMEM((n_sub, 2, DA_V_DIM + BF16_ROWS, tq), F32),
        ],
        compiler_params=_cparams(3),
        name="diff_attention",
    )(*args)


def _matmul_res_kernel(a_ref, w_ref, *refs, offs):
    r_refs, o_ref = refs[:-1], refs[-1]
    i = pl.program_id(0)
    y = jnp.dot(a_ref[...], w_ref[...], preferred_element_type=F32)
    for p, r_ref in enumerate(r_refs):
        @pl.when(_in_part(i, offs, p))
        def _(r_ref=r_ref):
            o_ref[...] = r_ref[...] + y


def _matmul_res(a, w, res_parts):
    t, k = a.shape
    n = w.shape[1]
    bm = _pick(math.gcd(*[p.shape[0] for p in res_parts]), RES_TILE[0])
    bn = _pick(n, RES_TILE[1])
    offs = _part_offsets(res_parts, bm)
    col = lambda i, j, lo, cnt: jnp.where(jnp.logical_and(i >= lo, i < lo + cnt), j, 0)
    return pl.pallas_call(
        functools.partial(_matmul_res_kernel, offs=tuple(offs)),
        grid=(t // bm, n // bn),
        in_specs=[
            pl.BlockSpec((bm, k), lambda i, j: (i, 0)),
            pl.BlockSpec((k, bn), lambda i, j: (0, j)),
        ] + [_part_spec((bm, bn), offs, p, col) for p in range(len(res_parts))],
        out_specs=pl.BlockSpec((bm, bn), lambda i, j: (i, j)),
        out_shape=jax.ShapeDtypeStruct((t, n), F32),
        compiler_params=_cparams(2),
        name="matmul_residual",
    )(a, w, *res_parts)


def _ffn_kernel(x_ref, nw_ref, wg_ref, wu_ref, wd_ref, nf_ref, *refs, offs, final_norm):
    o_refs, (h_ref, acc_ref) = refs[:-2], refs[-2:]
    i = pl.program_id(0)
    j = pl.program_id(1)

    @pl.when(j == 0)
    def _():
        h_ref[...] = _rms(x_ref[...], nw_ref[...]).astype(BF16)
        acc_ref[...] = jnp.zeros_like(acc_ref)

    h = h_ref[...]
    g = jnp.dot(h, wg_ref[...], preferred_element_type=F32)
    u = jnp.dot(h, wu_ref[...], preferred_element_type=F32)
    a = (_silu(g) * u).astype(BF16)
    acc_ref[...] += jnp.dot(a, wd_ref[...], preferred_element_type=F32)

    last = j == pl.num_programs(1) - 1
    for p, o_ref in enumerate(o_refs):
        @pl.when(jnp.logical_and(last, _in_part(i, offs, p)))
        def _(o_ref=o_ref):
            r = acc_ref[...] + x_ref[...]
            if final_norm:
                r = _rms(r, nf_ref[...])
            o_ref[...] = r


def _ffn(x, nw, w_gu, w_down, nf, final_norm, out_rows):
    t, d = x.shape
    dff = w_down.shape[0]
    bm = _pick(math.gcd(*out_rows), FFN_TILE[0])
    bf = _pick(dff, FFN_TILE[1])
    nf_tiles = dff // bf
    outs = [jax.ShapeDtypeStruct((r, d), F32) for r in out_rows]
    offs = _part_offsets(outs, bm)
    return pl.pallas_call(
        functools.partial(_ffn_kernel, offs=tuple(offs), final_norm=final_norm),
        grid=(t // bm, nf_tiles),
        in_specs=[
            pl.BlockSpec((bm, d), lambda i, j: (i, 0)),
            pl.BlockSpec((1, d), lambda i, j: (0, 0)),
            pl.BlockSpec((d, bf), lambda i, j: (0, j)),
            pl.BlockSpec((d, bf), lambda i, j: (0, j + nf_tiles)),
            pl.BlockSpec((bf, d), lambda i, j: (j, 0)),
            pl.BlockSpec((1, d), lambda i, j: (0, 0)),
        ],
        out_specs=[_part_spec((bm, d), offs, p, lambda i, j, lo, cnt: 0) for p in range(len(outs))],
        out_shape=outs,
        scratch_shapes=[pltpu.VMEM((bm, d), BF16), pltpu.VMEM((bm, d), F32)],
        compiler_params=_cparams(2),
        name="swiglu_ffn",
    )(x, nw, w_gu, w_gu, w_down, nf)


def _ssd_in_kernel(x_ref, nw_ref, w_ref, wdt_ref, o_ref, dt_ref, h_ref):
    j = pl.program_id(1)

    @pl.when(j == 0)
    def _():
        h = _rms(x_ref[...], nw_ref[...]).astype(BF16)
        h_ref[...] = h
        dt_ref[...] = jnp.dot(h, wdt_ref[...], preferred_element_type=F32)

    o_ref[...] = jnp.dot(h_ref[...], w_ref[...], preferred_element_type=F32).astype(o_ref.dtype)


def _ssd_in_proj(x, nw, w_zx, w_dt):
    t, d = x.shape
    n = w_zx.shape[1]
    ndt = w_dt.shape[1]
    bm = _pick(t, SSD_IN_TILE[0])
    bn = _pick(n, SSD_IN_TILE[1])
    return pl.pallas_call(
        _ssd_in_kernel,
        grid=(t // bm, n // bn),
        in_specs=[
            pl.BlockSpec((bm, d), lambda i, j: (i, 0)),
            pl.BlockSpec((1, d), lambda i, j: (0, 0)),
            pl.BlockSpec((d, bn), lambda i, j: (0, j)),
            pl.BlockSpec((d, ndt), lambda i, j: (0, 0)),
        ],
        out_specs=[
            pl.BlockSpec((bm, bn), lambda i, j: (i, j)),
            pl.BlockSpec((bm, ndt), lambda i, j: (i, 0)),
        ],
        out_shape=[jax.ShapeDtypeStruct((t, n), BF16), jax.ShapeDtypeStruct((t, ndt), F32)],
        scratch_shapes=[pltpu.VMEM((bm, d), BF16)],
        compiler_params=_cparams(2),
        name="ssd_in_proj",
    )(x, nw, w_zx, w_dt)


def _conv_kernel(flags_ref, prev_ref, cur_ref, next_ref, w_ref, b_ref, o_ref):
    f = flags_ref[pl.program_id(0)]
    at_start = (f & 1) == 1
    at_end = (f & 2) == 2
    x = cur_ref[...].astype(F32)
    bt = x.shape[0]
    row = lax.broadcasted_iota(jnp.int32, x.shape, 0)
    before = jnp.where(at_start, 0.0, prev_ref[SUBLANES - 1:SUBLANES, :].astype(F32))
    after = jnp.where(at_end, 0.0, next_ref[0:2, :].astype(F32))
    xm1 = jnp.where(row == 0, before, pltpu.roll(x, 1, 0))
    xp1 = jnp.where(row == bt - 1, after[0:1], pltpu.roll(x, bt - 1, 0))
    xp2 = jnp.where(row == bt - 2, after[0:1],
                    jnp.where(row == bt - 1, after[1:2], pltpu.roll(x, bt - 2, 0)))
    w = w_ref[...]
    y = xm1 * w[0:1] + x * w[1:2] + xp1 * w[2:3] + xp2 * w[3:4] + b_ref[...]
    o_ref[...] = _silu(y).astype(o_ref.dtype)


def _conv_silu(zx, conv_w, conv_b, seqs, col0):
    t = zx.shape[0]
    ncol = conv_w.shape[1]
    bt = _pick(math.gcd(*[s for _, s in seqs]), CONV_TILE[0])
    bc = _pick(ncol, CONV_TILE[1])
    nblk = t // bt
    starts = {r for r, _ in seqs}
    ends = {r + s for r, s in seqs}
    flags = np.array([(1 if i * bt in starts else 0) | (2 if (i + 1) * bt in ends else 0)
                      for i in range(nblk)], np.int32)
    hb = bt // SUBLANES
    c0 = col0 // bc
    grid_spec = pltpu.PrefetchScalarGridSpec(
        num_scalar_prefetch=1,
        grid=(nblk, ncol // bc),
        in_specs=[
            pl.BlockSpec((SUBLANES, bc), lambda i, j, f: (jnp.maximum(i * hb - 1, 0), c0 + j)),
            pl.BlockSpec((bt, bc), lambda i, j, f: (i, c0 + j)),
            pl.BlockSpec((SUBLANES, bc), lambda i, j, f: (jnp.minimum((i + 1) * hb, t // SUBLANES - 1), c0 + j)),
            pl.BlockSpec((4, bc), lambda i, j, f: (0, j)),
            pl.BlockSpec((1, bc), lambda i, j, f: (0, j)),
        ],
        out_specs=pl.BlockSpec((bt, bc), lambda i, j, f: (i, j)),
    )
    return pl.pallas_call(
        _conv_kernel,
        grid_spec=grid_spec,
        out_shape=jax.ShapeDtypeStruct((t, ncol), BF16),
        compiler_params=_cparams(2),
        name="ssd_conv_silu",
    )(jnp.asarray(flags), zx, zx, zx, conv_w, conv_b)


def _scan_kernel(flags_ref, xs_f, b_f, c_f, dt_f, xs_b, b_b, c_b, dt_b, bias_ref, alog_ref, d_ref,
                 yf_ref, yb_ref, st_f, ut_f, wt_f, st_b, ut_b, wt_b):
    f = flags_ref[pl.program_id(0)]

    @pl.when((f & 1) == 1)
    def _():
        st_f[...] = jnp.zeros_like(st_f)

    @pl.when((f & 2) == 2)
    def _():
        st_b[...] = jnp.zeros_like(st_b)

    _scan_direction(xs_f, b_f, c_f, dt_f, bias_ref, alog_ref, d_ref, yf_ref, st_f, ut_f, wt_f, reverse=False)
    _scan_direction(xs_b, b_b, c_b, dt_b, bias_ref, alog_ref, d_ref, yb_ref, st_b, ut_b, wt_b, reverse=True)


def _scan_direction(xs_ref, b_ref, c_ref, dt_ref, bias_ref, alog_ref, d_ref, y_ref, state_ref, ut_ref,
                    wt_ref, *, reverse):
    q = SSD_CHUNK
    dir_off = SSD_HEADS if reverse else 0
    xdt_in = dt_ref[...] + bias_ref[...]
    dt = jnp.maximum(xdt_in, 0.0) + jnp.log(1.0 + jnp.exp(-jnp.abs(xdt_in)))
    da = dt * (-jnp.exp(alog_ref[...]))
    row = lax.broadcasted_iota(jnp.int32, (q, LANES), 0)
    cum = da
    k = 1
    while k < q:
        cum = cum + jnp.where(row >= k, pltpu.roll(cum, k, 0), 0.0)
        k *= 2
    tot = cum[q - 1:q, :]
    u = (tot + da - cum) if reverse else cum
    state_decay = jnp.exp(tot)
    u2 = u * LOG2E
    ut_ref[...] = (u2 - jnp.log2(dt)).T
    wt_ref[...] = (dt * jnp.exp(tot - u)).T

    li = lax.broadcasted_iota(jnp.int32, (q, q), 0)
    si = lax.broadcasted_iota(jnp.int32, (q, q), 1)
    allowed = (si >= li) if reverse else (li >= si)
    low = lax.broadcasted_iota(jnp.int32, (q, LANES), 1) < SSD_HEAD_DIM

    for g in range(SSD_GROUPS):
        bg = b_ref[:, g * SSD_D_STATE:(g + 1) * SSD_D_STATE]
        cg = c_ref[:, g * SSD_D_STATE:(g + 1) * SSD_D_STATE]
        bt = bg.astype(F32).T
        cb = jnp.dot(cg, bt.astype(BF16), preferred_element_type=F32)
        st = state_ref[g]
        y_off = jnp.dot(cg, st.astype(BF16), preferred_element_type=F32)
        for j in range(SSD_HPG // 2):
            cols = slice(g * SSD_GROUP_W + j * LANES, g * SSD_GROUP_W + (j + 1) * LANES)
            tile = slice(j * LANES, (j + 1) * LANES)
            att, inj, scale = [], [], []
            for r in (2 * j, 2 * j + 1):
                h = dir_off + g * SSD_HPG + r
                ucol = jnp.broadcast_to(u2[:, h:h + 1], (q, q))
                decay_dt = jnp.exp2(jnp.where(allowed, ucol - ut_ref[h:h + 1, :], -jnp.inf))
                att.append((cb * decay_dt).astype(BF16))
                inj.append((bt * wt_ref[h:h + 1, :]).astype(BF16))
                scale.append(jnp.exp2(ucol))
            xt = xs_ref[:, cols]
            zt = jnp.zeros_like(xt)
            rhs = jnp.concatenate([jnp.where(low, xt, zt), jnp.where(low, zt, xt)], axis=0)
            lhs = jnp.concatenate([jnp.concatenate(att, axis=1), jnp.concatenate(inj, axis=1)], axis=0)
            res = jnp.dot(lhs, rhs, preferred_element_type=F32)
            y = res[:q] + y_off[:, tile] * jnp.where(low, scale[0], scale[1])
            if not reverse:
                y = y + d_ref[:, cols] * xt.astype(F32)
            y_ref[:, cols] = y.astype(y_ref.dtype)
            h = dir_off + g * SSD_HPG + 2 * j
            keep = jnp.where(low[0:1], state_decay[:, h:h + 1], state_decay[:, h + 1:h + 2])
            state_ref[g, :, tile] = st[:, tile] * keep + res[q:]


def _ssd_scan(xbc, dt_raw, dt_bias, a_log, d_row, seqs):
    t = xbc.shape[0]
    q = SSD_CHUNK
    nchunk = t // q
    starts = {r // q for r, _ in seqs}
    ends = {(r + s) // q - 1 for r, s in seqs}
    flags = np.array([(1 if c in starts else 0) | (2 if nchunk - 1 - c in ends else 0)
                      for c in range(nchunk)], np.int32)
    fwd = lambda c: c
    bwd = lambda c: nchunk - 1 - c
    nxb = SSD_D_INNER // SSD_BC_W

    def chunk_specs(blk):
        return [
            pl.BlockSpec((q, SSD_D_INNER), lambda c, f: (blk(c), 0)),
            pl.BlockSpec((q, SSD_BC_W), lambda c, f: (blk(c), nxb)),
            pl.BlockSpec((q, SSD_BC_W), lambda c, f: (blk(c), nxb + 1)),
            pl.BlockSpec((q, 2 * SSD_HEADS), lambda c, f: (blk(c), 0)),
        ]

    per_direction = [pltpu.VMEM((SSD_GROUPS, SSD_D_STATE, SSD_GROUP_W), F32),
                     pltpu.VMEM((2 * SSD_HEADS, q), F32),
                     pltpu.VMEM((2 * SSD_HEADS, q), F32)]
    grid_spec = pltpu.PrefetchScalarGridSpec(
        num_scalar_prefetch=1,
        grid=(nchunk,),
        in_specs=chunk_specs(fwd) + chunk_specs(bwd) + [
            pl.BlockSpec((1, 2 * SSD_HEADS), lambda c, f: (0, 0)),
            pl.BlockSpec((1, 2 * SSD_HEADS), lambda c, f: (0, 0)),
            pl.BlockSpec((1, SSD_D_INNER), lambda c, f: (0, 0)),
        ],
        out_specs=[pl.BlockSpec((q, SSD_D_INNER), lambda c, f: (fwd(c), 0)),
                   pl.BlockSpec((q, SSD_D_INNER), lambda c, f: (bwd(c), 0))],
        scratch_shapes=per_direction + per_direction,
    )
    out = jax.ShapeDtypeStruct((t, SSD_D_INNER), BF16)
    return pl.pallas_call(
        _scan_kernel,
        grid_spec=grid_spec,
        out_shape=[out, out],
        compiler_params=_cparams(1),
        name="ssd_scan",
    )(jnp.asarray(flags), xbc, xbc, xbc, dt_raw, xbc, xbc, xbc, dt_raw, dt_bias, a_log, d_row)


def _ssd_out_kernel(yf_ref, yb_ref, z_ref, nw_ref, w_ref, r_ref, o_ref):
    acc = r_ref[...]
    for g in range(SSD_GROUPS):
        cols = slice(g * SSD_GROUP_W, (g + 1) * SSD_GROUP_W)
        y = yf_ref[:, cols].astype(F32) + yb_ref[:, cols].astype(F32)
        gated = y * _silu(z_ref[:, cols].astype(F32))
        yn = _rms(gated, nw_ref[:, cols]).astype(BF16)
        acc = acc + jnp.dot(yn, w_ref[cols, :], preferred_element_type=F32)
    o_ref[...] = acc


def _ssd_out_proj(y_fwd, y_bwd, zx, nw, w, res):
    t, k = y_fwd.shape
    n = w.shape[1]
    bm = _pick(t, SSD_OUT_ROWS)
    return pl.pallas_call(
        _ssd_out_kernel,
        grid=(t // bm,),
        in_specs=[
            pl.BlockSpec((bm, k), lambda i: (i, 0)),
            pl.BlockSpec((bm, k), lambda i: (i, 0)),
            pl.BlockSpec((bm, k), lambda i: (i, 0)),
            pl.BlockSpec((1, k), lambda i: (0, 0)),
            pl.BlockSpec((k, n), lambda i: (0, 0), pipeline_mode=pl.Buffered(1)),
            pl.BlockSpec((bm, n), lambda i: (i, 0)),
        ],
        out_specs=pl.BlockSpec((bm, n), lambda i: (i, 0)),
        out_shape=jax.ShapeDtypeStruct((t, n), F32),
        compiler_params=_cparams(1),
        name="ssd_out_proj",
    )(y_fwd, y_bwd, zx, nw, w, res)


def _rope_tables(seqs, t):
    half = DA_HEAD_DIM // 2
    inv = 1.0 / (ROPE_THETA ** (jnp.arange(0, DA_HEAD_DIM, 2, dtype=F32) / DA_HEAD_DIM))
    pos = np.zeros((t,), np.float32)
    for r, s in seqs:
        pos[r:r + s] = np.arange(s, dtype=np.float32)
    ang = jnp.asarray(pos)[:, None] * inv[None, :]
    cos, sin = jnp.cos(ang), jnp.sin(ang)
    cos_t = jnp.concatenate([cos, cos, cos, cos], axis=1)
    sin_t = jnp.concatenate([-sin, -sin, sin, sin], axis=1)
    assert cos_t.shape == (t, LANES) and 4 * half == LANES
    return cos_t, sin_t


def _reorder_qk_columns(w_qkv):
    d = w_qkv.shape[0]
    half = DA_HEAD_DIM // 2
    qk = w_qkv[:, :2 * d].reshape(d, 2 * d // LANES, 2, 2, half)
    qk = qk.transpose(0, 1, 3, 2, 4).reshape(d, 2 * d)
    return jnp.concatenate([qk, w_qkv[:, 2 * d:]], axis=1)


def _seq_groups(seqs):
    groups = []
    for r, s in seqs:
        if groups and groups[-1][2] == s and groups[-1][0] + groups[-1][1] * s == r:
            groups[-1][1] += 1
        else:
            groups.append([r, 1, s])
    return [tuple(g) for g in groups]


def _attention_layer(x_parts, seqs, nw, w_qkv, w_o, lam_p, sub_w, lambda_init):
    t = sum(p.shape[0] for p in x_parts)
    cos_t, sin_t = _rope_tables(seqs, t)
    qkv = _qkv_proj(x_parts, nw, _reorder_qk_columns(w_qkv).astype(BF16), cos_t, sin_t)
    o = None
    for r, n, s in _seq_groups(seqs):
        o = _attention_group(qkv, lam_p, sub_w, r, n, s, lambda_init, o)
    return _matmul_res(o, w_o.astype(BF16), x_parts)


def _ssd_layer(x, seqs, nw, w_in, conv_w, conv_b, dt_bias, a_log, d_skip, norm_w, w_out):
    nzx = SSD_D_INNER + SSD_CONV_DIM
    zx, dt_raw = _ssd_in_proj(x, nw, w_in[:, :nzx].astype(BF16), w_in[:, nzx:].astype(BF16))
    xbc = _conv_silu(zx, conv_w, conv_b.reshape(1, -1), seqs, SSD_D_INNER)
    bias = dt_bias.reshape(1, -1)
    alog = a_log.reshape(1, -1)
    d_row = jnp.repeat(d_skip, SSD_HEAD_DIM).reshape(1, -1)
    y_fwd, y_bwd = _ssd_scan(xbc, dt_raw, bias, alog, d_row, seqs)
    return _ssd_out_proj(y_fwd, y_bwd, zx, norm_w.reshape(1, -1), w_out.astype(BF16), x)


def _trunk(x_parts, seqs, norm_mix, norm_ffn, norm_final, da_w_qkv, da_w_o, da_lambda_q1, da_lambda_k1,
           da_lambda_q2, da_lambda_k2, da_subln, ssd_w_in, ssd_conv_w, ssd_conv_b, ssd_dt_bias,
           ssd_a_log, ssd_d, ssd_norm, ssd_w_out, ffn_w_gu, ffn_w_down):
    depth = norm_mix.shape[0]
    nf = norm_final.reshape(1, -1)
    part_rows = [p.shape[0] for p in x_parts]
    x = x_parts
    for i in range(depth):
        j = i // 2
        nw = norm_mix[i].reshape(1, -1)
        if i % 2 == 0:
            lambda_init = 0.8 - 0.6 * math.exp(-0.3 * i)
            lam_p = jnp.stack([da_lambda_q1[j], da_lambda_k1[j], da_lambda_q2[j], da_lambda_k2[j]])
            y = _attention_layer(x, seqs, nw, da_w_qkv[j], da_w_o[j], lam_p,
                                 da_subln[j].reshape(1, -1), lambda_init)
        else:
            (x0,) = x
            y = _ssd_layer(x0, seqs, nw, ssd_w_in[j], ssd_conv_w[j], ssd_conv_b[j], ssd_dt_bias[j],
                           ssd_a_log[j], ssd_d[j], ssd_norm[j], ssd_w_out[j])
        last = i == depth - 1
        x = tuple(_ffn(y, norm_ffn[i].reshape(1, -1), ffn_w_gu[i].astype(BF16), ffn_w_down[i].astype(BF16),
                       nf, final_norm=last, out_rows=part_rows if last else [y.shape[0]]))
    return x


def kernel(x_prompt, x_sample, norm_mix, norm_ffn, norm_final, da_w_qkv, da_w_o, da_lambda_q1, da_lambda_k1, da_lambda_q2, da_lambda_k2, da_subln, ssd_w_in, ssd_conv_w, ssd_conv_b, ssd_dt_bias, ssd_a_log, ssd_d, ssd_norm, ssd_w_out, ffn_w_gu, ffn_w_down):
    d = x_prompt.shape[-1]
    seqs = []
    for arr in (x_prompt, x_sample):
        for _ in range(arr.shape[0]):
            seqs.append((sum(s for _, s in seqs), arr.shape[1]))
    x_parts = (x_prompt.reshape(-1, d), x_sample.reshape(-1, d))
    y_p, y_s = _trunk(x_parts, tuple(seqs), norm_mix, norm_ffn, norm_final, da_w_qkv, da_w_o,
                      da_lambda_q1, da_lambda_k1, da_lambda_q2, da_lambda_k2, da_subln, ssd_w_in,
                      ssd_conv_w, ssd_conv_b, ssd_dt_bias, ssd_a_log, ssd_d, ssd_norm, ssd_w_out,
                      ffn_w_gu, ffn_w_down)
    return y_p.reshape(x_prompt.shape), y_s.reshape(x_sample.shape)
```

```python
import functools
import math

import numpy as np
import jax
import jax.numpy as jnp
from jax import lax
from jax.experimental import pallas as pl
from jax.experimental.pallas import tpu as pltpu

F32 = jnp.float32
BF16 = jnp.bfloat16

EPS = 1e-5
LOG2E = 1.4426950408889634
LANES = 128
SUBLANES = 8
BF16_ROWS = 2 * SUBLANES
VMEM_LIMIT = 56 * 1024 * 1024

QKV_TILE = (512, 2048)
RES_TILE = (1024, 1024)
FFN_TILE = (512, 512)
SSD_IN_TILE = (1024, 1024)
CONV_TILE = (512, 1024)
CONV_SHIFT_ROWS = 256
SSD_OUT_ROWS = 256
ATTN_TQ = 512
ATTN_TK = 1024
ATTN_MAX_UNROLLED_STAGES = 8

DA_HEADS = 16
DA_HEAD_DIM = 64
DA_V_DIM = 2 * DA_HEAD_DIM
ROPE_THETA = 10000.0

SSD_HEAD_DIM = 64
SSD_GROUPS = 8
SSD_HPG = 8
SSD_HEADS = SSD_GROUPS * SSD_HPG
SSD_D_STATE = 128
SSD_D_INNER = SSD_HEADS * SSD_HEAD_DIM
SSD_CHUNK = 128
SSD_GROUP_W = SSD_HPG * SSD_HEAD_DIM
SSD_BC_W = SSD_GROUPS * SSD_D_STATE
SSD_CONV_DIM = SSD_D_INNER + 2 * SSD_BC_W


def _cparams(n_axes):
    return pltpu.CompilerParams(dimension_semantics=("arbitrary",) * n_axes,
                                vmem_limit_bytes=VMEM_LIMIT)


def _rms(x, w):
    ms = jnp.mean(x * x, axis=-1, keepdims=True)
    return x * lax.rsqrt(ms + EPS) * w


def _silu(x):
    return x / (1.0 + jnp.exp(-x))


def _pick(n, pref):
    b = min(n, pref)
    while n % b:
        b //= 2
    return b


def _part_offsets(parts, bm):
    offs = [0]
    for p in parts:
        assert p.shape[0] % bm == 0
        offs.append(offs[-1] + p.shape[0] // bm)
    return offs


def _part_spec(block, offs, p, col_fn):
    lo, n = offs[p], offs[p + 1] - offs[p]
    return pl.BlockSpec(block, lambda i, j: (jnp.clip(i - lo, 0, n - 1), col_fn(i, j, lo, n)))


def _in_part(i, offs, p):
    return jnp.logical_and(i >= offs[p], i < offs[p + 1])


def _qkv_kernel(*refs, offs, n_q_tiles, n_rope_tiles):
    nparts = len(offs) - 1
    x_refs = refs[:nparts]
    nw_ref, w_ref, cos_ref, sin_ref, o_ref, h_ref = refs[nparts:]
    i = pl.program_id(0)
    j = pl.program_id(1)

    for p in range(nparts):
        @pl.when(jnp.logical_and(j == 0, _in_part(i, offs, p)))
        def _(p=p):
            h_ref[...] = _rms(x_refs[p][...], nw_ref[...]).astype(BF16)

    y = jnp.dot(h_ref[...], w_ref[...], preferred_element_type=F32)
    rope = j < n_rope_tiles
    c = jnp.where(rope, cos_ref[...], 1.0)
    s = jnp.where(rope, sin_ref[...], 0.0)
    scale = jnp.where(j < n_q_tiles, LOG2E * DA_HEAD_DIM ** -0.5, 1.0)
    c = c * scale
    s = s * scale
    for t in range(y.shape[1] // LANES):
        yt = y[:, t * LANES:(t + 1) * LANES]
        o_ref[:, t * LANES:(t + 1) * LANES] = (yt * c + pltpu.roll(yt, LANES // 2, 1) * s).astype(o_ref.dtype)


def _qkv_proj(x_parts, nw, w, cos_t, sin_t):
    d = x_parts[0].shape[1]
    t = sum(p.shape[0] for p in x_parts)
    n = w.shape[1]
    bm = _pick(math.gcd(*[p.shape[0] for p in x_parts]), QKV_TILE[0])
    bn = _pick(d, QKV_TILE[1])
    offs = _part_offsets(x_parts, bm)
    return pl.pallas_call(
        functools.partial(_qkv_kernel, offs=tuple(offs), n_q_tiles=d // bn, n_rope_tiles=2 * d // bn),
        grid=(t // bm, n // bn),
        in_specs=[_part_spec((bm, d), offs, p, lambda i, j, lo, cnt: 0) for p in range(len(x_parts))] + [
            pl.BlockSpec((1, d), lambda i, j: (0, 0)),
            pl.BlockSpec((d, bn), lambda i, j: (0, j)),
            pl.BlockSpec((bm, LANES), lambda i, j: (i, 0)),
            pl.BlockSpec((bm, LANES), lambda i, j: (i, 0)),
        ],
        out_specs=pl.BlockSpec((bm, bn), lambda i, j: (i, j)),
        out_shape=jax.ShapeDtypeStruct((t, n), BF16),
        scratch_shapes=[pltpu.VMEM((bm, d), BF16)],
        compiler_params=_cparams(2),
        name="qkv_proj",
    )(*x_parts, nw, w, cos_t, sin_t)


def _attn_kernel(lam_ref, sub_ref, q_ref, k_ref, v_ref, o_ref, vt_ref, qt_ref, s_ref, smax_ref,
                 m_ref, acc_ref, *, tk, tq, lambda_init):
    n_sub = q_ref.shape[0] // tq
    nk = k_ref.shape[0] // tk

    @pl.when(pl.program_id(2) == 0)
    def _():
        def transpose_chunk(c, carry):
            start = pl.multiple_of(c * tk, tk)
            vt_ref[0:DA_V_DIM, pl.ds(start, tk)] = v_ref[pl.ds(start, tk), :].astype(F32).T.astype(BF16)
            return carry
        lax.fori_loop(0, nk, transpose_chunk, 0)
        extra = lax.broadcasted_iota(jnp.int32, (vt_ref.shape[0] - DA_V_DIM, vt_ref.shape[1]), 0)
        vt_ref[DA_V_DIM:, :] = jnp.where(extra == 0, 1.0, 0.0).astype(BF16)

    row = lax.broadcasted_iota(jnp.int32, (DA_V_DIM, tq), 0)
    in_map0 = (row % DA_HEAD_DIM) < (DA_HEAD_DIM // 2)
    for b in range(n_sub):
        qt = q_ref[b * tq:(b + 1) * tq, :].astype(F32).T
        qt_ref[b, 0] = jnp.where(in_map0, qt, 0.0).astype(BF16)
        qt_ref[b, 1] = jnp.where(in_map0, 0.0, qt).astype(BF16)
    m_ref[...] = jnp.full(m_ref.shape, -jnp.inf, F32)
    acc_ref[...] = jnp.zeros(acc_ref.shape, F32)

    def tile_start(tile):
        return tile * tk if isinstance(tile, int) else pl.multiple_of(tile * tk, tk)

    def scores(b, tile, slot):
        k = k_ref[pl.ds(tile_start(tile), tk), :]
        for mp in range(2):
            st = jnp.dot(k, qt_ref[b, mp], preferred_element_type=F32)
            s_ref[slot, mp] = st
            smax_ref[slot, mp] = jnp.max(st, axis=0, keepdims=True)

    def update(b, tile, slot):
        vt = vt_ref[:, pl.ds(tile_start(tile), tk)]
        for mp in range(2):
            m_old = m_ref[b, mp]
            m_new = jnp.maximum(m_old, smax_ref[slot, mp])
            alpha = jnp.exp2(m_old - m_new)
            p = jnp.exp2(s_ref[slot, mp] - m_new)
            acc_ref[b, mp] = alpha * acc_ref[b, mp] + jnp.dot(vt, p.astype(BF16), preferred_element_type=F32)
            m_ref[b, mp] = m_new

    n_stages = n_sub * nk
    if n_stages > ATTN_MAX_UNROLLED_STAGES:
        def at(u):
            return (u // nk, u % nk) if isinstance(u, int) else (lax.div(u, nk), lax.rem(u, nk))

        scores(0, 0, 0)

        def stage_pair(j):
            u = 2 * j
            scores(*at(u + 1), 1)
            update(*at(u), 0)
            scores(*at(u + 2), 0)
            update(*at(u + 1), 1)

        def body(i, carry):
            stage_pair(2 * i)
            stage_pair(2 * i + 1)
            return carry

        n_pairs = n_stages // 2 - 1
        lax.fori_loop(0, n_pairs // 2, body, 0)
        if n_pairs % 2:
            stage_pair(n_pairs - 1)
        scores(*at(n_stages - 1), 1)
        update(*at(n_stages - 2), 0)
        update(*at(n_stages - 1), 1)
    else:
        stages = [(b, t) for b in range(n_sub) for t in range(nk)]
        scores(*stages[0], 0)
        for i in range(len(stages) - 1):
            scores(*stages[i + 1], (i + 1) % 2)
            update(*stages[i], i % 2)
        update(*stages[-1], (len(stages) - 1) % 2)

    lam_p = lam_ref[...]
    t1 = jnp.sum(lam_p[0:1, :] * lam_p[1:2, :], axis=-1, keepdims=True)
    t2 = jnp.sum(lam_p[2:3, :] * lam_p[3:4, :], axis=-1, keepdims=True)
    lam = jnp.exp(t1) - jnp.exp(t2) + lambda_init
    for b in range(n_sub):
        acc0, acc1 = acc_ref[b, 0], acc_ref[b, 1]
        ot = (acc0[:DA_V_DIM] / acc0[DA_V_DIM:DA_V_DIM + 1]
              - lam * (acc1[:DA_V_DIM] / acc1[DA_V_DIM:DA_V_DIM + 1]))
        ms = jnp.mean(ot * ot, axis=0, keepdims=True)
        o = (ot * lax.rsqrt(ms + EPS)).T
        o_ref[b * tq:(b + 1) * tq, :] = (o * sub_ref[...] * (1.0 - lambda_init)).astype(o_ref.dtype)


def _attn_alias_kernel(lam_ref, sub_ref, q_ref, k_ref, v_ref, prev_ref, o_ref, *scratch, **kw):
    del prev_ref
    _attn_kernel(lam_ref, sub_ref, q_ref, k_ref, v_ref, o_ref, *scratch, **kw)


def _attention_group(qkv, lam_p, sub_w, row0, nseq, s, lambda_init, prev_out):
    tq = _pick(s, ATTN_TQ)
    tk = _pick(s // 2, ATTN_TK)
    if (s // tq) * (s // tk) <= ATTN_MAX_UNROLLED_STAGES:
        n_sub = s // tq
    else:
        tq, n_sub = tq // 2, 2
    bq = n_sub * tq
    nq = s // bq
    assert row0 % s == 0 and (s // tk) % 2 == 0
    kw = dict(tk=tk, tq=tq, lambda_init=lambda_init)
    in_specs = [
        pl.BlockSpec((4, DA_HEAD_DIM), lambda b, h, i: (0, 0)),
        pl.BlockSpec((1, DA_V_DIM), lambda b, h, i: (0, 0)),
        pl.BlockSpec((bq, DA_V_DIM), lambda b, h, i: (row0 // bq + b * nq + i, h)),
        pl.BlockSpec((s, DA_V_DIM), lambda b, h, i: (row0 // s + b, DA_HEADS + h)),
        pl.BlockSpec((s, DA_V_DIM), lambda b, h, i: (row0 // s + b, 2 * DA_HEADS + h)),
    ]
    args = [lam_p, sub_w, qkv, qkv, qkv]
    if prev_out is None:
        kern, aliases = functools.partial(_attn_kernel, **kw), {}
    else:
        kern, aliases = functools.partial(_attn_alias_kernel, **kw), {len(args): 0}
        in_specs.append(pl.BlockSpec(memory_space=pl.ANY))
        args.append(prev_out)
    return pl.pallas_call(
        kern,
        grid=(nseq, DA_HEADS, nq),
        in_specs=in_specs,
        out_specs=pl.BlockSpec((bq, DA_V_DIM), lambda b, h, i: (row0 // bq + b * nq + i, h)),
        out_shape=jax.ShapeDtypeStruct((qkv.shape[0], DA_HEADS * DA_V_DIM), BF16),
        input_output_aliases=aliases,
        scratch_shapes=[
            pltpu.VMEM((DA_V_DIM + BF16_ROWS, s), BF16),
            pltpu.VMEM((n_sub, 2, DA_V_DIM, tq), BF16),
            pltpu.VMEM((2, 2, tk, tq), F32),
            pltpu.VMEM((2, 2, 1, tq), F32),
            pltpu.VMEM((n_sub, 2, 1, tq), F32),
            pltpu.VMEM((n_sub, 2, DA_V_DIM + BF16_ROWS, tq), F32),
        ],
        compiler_params=_cparams(3),
        name="diff_attention",
    )(*args)


def _matmul_res_kernel(a_ref, w_ref, *refs, offs):
    r_refs, o_ref = refs[:-1], refs[-1]
    i = pl.program_id(0)
    y = jnp.dot(a_ref[...], w_ref[...], preferred_element_type=F32)
    for p, r_ref in enumerate(r_refs):
        @pl.when(_in_part(i, offs, p))
        def _(r_ref=r_ref):
            o_ref[...] = r_ref[...] + y


def _matmul_res(a, w, res_parts):
    t, k = a.shape
    n = w.shape[1]
    bm = _pick(math.gcd(*[p.shape[0] for p in res_parts]), RES_TILE[0])
    bn = _pick(n, RES_TILE[1])
    offs = _part_offsets(res_parts, bm)
    col = lambda i, j, lo, cnt: jnp.where(jnp.logical_and(i >= lo, i < lo + cnt), j, 0)
    return pl.pallas_call(
        functools.partial(_matmul_res_kernel, offs=tuple(offs)),
        grid=(t // bm, n // bn),
        in_specs=[
            pl.BlockSpec((bm, k), lambda i, j: (i, 0)),
            pl.BlockSpec((k, bn), lambda i, j: (0, j)),
        ] + [_part_spec((bm, bn), offs, p, col) for p in range(len(res_parts))],
        out_specs=pl.BlockSpec((bm, bn), lambda i, j: (i, j)),
        out_shape=jax.ShapeDtypeStruct((t, n), F32),
        compiler_params=_cparams(2),
        name="matmul_residual",
    )(a, w, *res_parts)


def _ffn_kernel(x_ref, nw_ref, wg_ref, wu_ref, wd_ref, nf_ref, *refs, offs, final_norm):
    o_refs, (h_ref, acc_ref) = refs[:-2], refs[-2:]
    i = pl.program_id(0)
    j = pl.program_id(1)

    @pl.when(j == 0)
    def _():
        h_ref[...] = _rms(x_ref[...], nw_ref[...]).astype(BF16)
        acc_ref[...] = jnp.zeros_like(acc_ref)

    h = h_ref[...]
    g = jnp.dot(h, wg_ref[...], preferred_element_type=F32)
    u = jnp.dot(h, wu_ref[...], preferred_element_type=F32)
    a = (_silu(g) * u).astype(BF16)
    acc_ref[...] += jnp.dot(a, wd_ref[...], preferred_element_type=F32)

    last = j == pl.num_programs(1) - 1
    for p, o_ref in enumerate(o_refs):
        @pl.when(jnp.logical_and(last, _in_part(i, offs, p)))
        def _(o_ref=o_ref):
            r = acc_ref[...] + x_ref[...]
            if final_norm:
                r = _rms(r, nf_ref[...])
            o_ref[...] = r


def _ffn(x, nw, w_gu, w_down, nf, final_norm, out_rows):
    t, d = x.shape
    dff = w_down.shape[0]
    bm = _pick(math.gcd(*out_rows), FFN_TILE[0])
    bf = _pick(dff, FFN_TILE[1])
    nf_tiles = dff // bf
    outs = [jax.ShapeDtypeStruct((r, d), F32) for r in out_rows]
    offs = _part_offsets(outs, bm)
    return pl.pallas_call(
        functools.partial(_ffn_kernel, offs=tuple(offs), final_norm=final_norm),
        grid=(t // bm, nf_tiles),
        in_specs=[
            pl.BlockSpec((bm, d), lambda i, j: (i, 0)),
            pl.BlockSpec((1, d), lambda i, j: (0, 0)),
            pl.BlockSpec((d, bf), lambda i, j: (0, j)),
            pl.BlockSpec((d, bf), lambda i, j: (0, j + nf_tiles)),
            pl.BlockSpec((bf, d), lambda i, j: (j, 0)),
            pl.BlockSpec((1, d), lambda i, j: (0, 0)),
        ],
        out_specs=[_part_spec((bm, d), offs, p, lambda i, j, lo, cnt: 0) for p in range(len(outs))],
        out_shape=outs,
        scratch_shapes=[pltpu.VMEM((bm, d), BF16), pltpu.VMEM((bm, d), F32)],
        compiler_params=_cparams(2),
        name="swiglu_ffn",
    )(x, nw, w_gu, w_gu, w_down, nf)


def _ssd_in_kernel(x_ref, nw_ref, w_ref, wdt_ref, o_ref, dt_ref, h_ref):
    j = pl.program_id(1)

    @pl.when(j == 0)
    def _():
        h = _rms(x_ref[...], nw_ref[...]).astype(BF16)
        h_ref[...] = h
        dt_ref[...] = jnp.dot(h, wdt_ref[...], preferred_element_type=F32)

    o_ref[...] = jnp.dot(h_ref[...], w_ref[...], preferred_element_type=F32).astype(o_ref.dtype)


def _ssd_in_proj(x, nw, w_zx, w_dt):
    t, d = x.shape
    n = w_zx.shape[1]
    ndt = w_dt.shape[1]
    bm = _pick(t, SSD_IN_TILE[0])
    bn = _pick(n, SSD_IN_TILE[1])
    return pl.pallas_call(
        _ssd_in_kernel,
        grid=(t // bm, n // bn),
        in_specs=[
            pl.BlockSpec((bm, d), lambda i, j: (i, 0)),
            pl.BlockSpec((1, d), lambda i, j: (0, 0)),
            pl.BlockSpec((d, bn), lambda i, j: (0, j)),
            pl.BlockSpec((d, ndt), lambda i, j: (0, 0)),
        ],
        out_specs=[
            pl.BlockSpec((bm, bn), lambda i, j: (i, j)),
            pl.BlockSpec((bm, ndt), lambda i, j: (i, 0)),
        ],
        out_shape=[jax.ShapeDtypeStruct((t, n), BF16), jax.ShapeDtypeStruct((t, ndt), F32)],
        scratch_shapes=[pltpu.VMEM((bm, d), BF16)],
        compiler_params=_cparams(2),
        name="ssd_in_proj",
    )(x, nw, w_zx, w_dt)


def _conv_kernel(flags_ref, prev_ref, cur_ref, next_ref, w_ref, b_ref, o_ref, *, sub):
    f = flags_ref[pl.program_id(0)]
    bt = cur_ref.shape[0]
    before = jnp.where((f & 1) == 1, 0.0, prev_ref[SUBLANES - 1:SUBLANES, :].astype(F32))
    after = jnp.where((f & 2) == 2, 0.0, next_ref[0:2, :].astype(F32))
    w = w_ref[...]
    b = b_ref[...]
    r = lax.broadcasted_iota(jnp.int32, (3 * sub, sub), 0)
    c = lax.broadcasted_iota(jnp.int32, (3 * sub, sub), 1)
    src = jnp.where(r < sub, r - 1, jnp.where(r < 2 * sub, r - sub + 1, r - 2 * sub + 2))
    shifts = jnp.where(c == src, 1.0, 0.0).astype(BF16)
    row = lax.broadcasted_iota(jnp.int32, (SUBLANES, cur_ref.shape[1]), 0)
    for i in range(bt // sub):
        lo, hi = i * sub, (i + 1) * sub
        xb = cur_ref[lo:hi, :]
        sh = jnp.dot(shifts, xb, preferred_element_type=F32)
        y = (sh[0:sub] * w[0:1] + xb.astype(F32) * w[1:2] + sh[sub:2 * sub] * w[2:3]
             + sh[2 * sub:3 * sub] * w[3:4] + b)
        left = before if i == 0 else cur_ref[lo - BF16_ROWS:lo, :][BF16_ROWS - 1:BF16_ROWS].astype(F32)
        right = after if hi == bt else cur_ref[hi:hi + BF16_ROWS, :][0:2].astype(F32)
        first = y[0:SUBLANES] + jnp.where(row == 0, left * w[0:1], 0.0)
        last = (y[sub - SUBLANES:sub]
                + jnp.where(row == SUBLANES - 1, right[0:1] * w[2:3] + right[1:2] * w[3:4], 0.0)
                + jnp.where(row == SUBLANES - 2, right[0:1] * w[3:4], 0.0))
        y = jnp.concatenate([first, y[SUBLANES:sub - SUBLANES], last], axis=0)
        o_ref[lo:hi, :] = _silu(y).astype(o_ref.dtype)


def _conv_silu(zx, conv_w, conv_b, seqs, col0):
    t = zx.shape[0]
    ncol = conv_w.shape[1]
    bt = _pick(math.gcd(*[s for _, s in seqs]), CONV_TILE[0])
    bc = _pick(ncol, CONV_TILE[1])
    nblk = t // bt
    starts = {r for r, _ in seqs}
    ends = {r + s for r, s in seqs}
    flags = np.array([(1 if i * bt in starts else 0) | (2 if (i + 1) * bt in ends else 0)
                      for i in range(nblk)], np.int32)
    hb = bt // SUBLANES
    c0 = col0 // bc
    grid_spec = pltpu.PrefetchScalarGridSpec(
        num_scalar_prefetch=1,
        grid=(nblk, ncol // bc),
        in_specs=[
            pl.BlockSpec((SUBLANES, bc), lambda i, j, f: (jnp.maximum(i * hb - 1, 0), c0 + j)),
            pl.BlockSpec((bt, bc), lambda i, j, f: (i, c0 + j)),
            pl.BlockSpec((SUBLANES, bc), lambda i, j, f: (jnp.minimum((i + 1) * hb, t // SUBLANES - 1), c0 + j)),
            pl.BlockSpec((4, bc), lambda i, j, f: (0, j)),
            pl.BlockSpec((1, bc), lambda i, j, f: (0, j)),
        ],
        out_specs=pl.BlockSpec((bt, bc), lambda i, j, f: (i, j)),
    )
    return pl.pallas_call(
        functools.partial(_conv_kernel, sub=_pick(bt, CONV_SHIFT_ROWS)),
        grid_spec=grid_spec,
        out_shape=jax.ShapeDtypeStruct((t, ncol), BF16),
        compiler_params=_cparams(2),
        name="ssd_conv_silu",
    )(jnp.asarray(flags), zx, zx, zx, conv_w, conv_b)


def _scan_kernel(flags_ref, xs_f, b_f, c_f, dt_f, xs_b, b_b, c_b, dt_b, bias_ref, alog_ref, d_ref,
                 yf_ref, yb_ref, st_f, ut_f, wt_f, st_b, ut_b, wt_b):
    f = flags_ref[pl.program_id(0)]

    @pl.when((f & 1) == 1)
    def _():
        st_f[...] = jnp.zeros_like(st_f)

    @pl.when((f & 2) == 2)
    def _():
        st_b[...] = jnp.zeros_like(st_b)

    _scan_direction(xs_f, b_f, c_f, dt_f, bias_ref, alog_ref, d_ref, yf_ref, st_f, ut_f, wt_f, reverse=False)
    _scan_direction(xs_b, b_b, c_b, dt_b, bias_ref, alog_ref, d_ref, yb_ref, st_b, ut_b, wt_b, reverse=True)


def _scan_direction(xs_ref, b_ref, c_ref, dt_ref, bias_ref, alog_ref, d_ref, y_ref, state_ref, ut_ref,
                    wt_ref, *, reverse):
    q = SSD_CHUNK
    dir_off = SSD_HEADS if reverse else 0
    xdt_in = dt_ref[...] + bias_ref[...]
    dt = jnp.maximum(xdt_in, 0.0) + jnp.log(1.0 + jnp.exp(-jnp.abs(xdt_in)))
    da = dt * (-jnp.exp(alog_ref[...]))
    row = lax.broadcasted_iota(jnp.int32, (q, LANES), 0)
    cum = da
    k = 1
    while k < q:
        cum = cum + jnp.where(row >= k, pltpu.roll(cum, k, 0), 0.0)
        k *= 2
    tot = cum[q - 1:q, :]
    u = (tot + da - cum) if reverse else cum
    state_decay = jnp.exp(tot)
    u2 = u * LOG2E
    ut_ref[...] = (u2 - jnp.log2(dt)).T
    wt_ref[...] = (dt * jnp.exp(tot - u)).T

    li = lax.broadcasted_iota(jnp.int32, (q, q), 0)
    si = lax.broadcasted_iota(jnp.int32, (q, q), 1)
    allowed = (si >= li) if reverse else (li >= si)
    low = lax.broadcasted_iota(jnp.int32, (q, LANES), 1) < SSD_HEAD_DIM

    for g in range(SSD_GROUPS):
        bg = b_ref[:, g * SSD_D_STATE:(g + 1) * SSD_D_STATE]
        cg = c_ref[:, g * SSD_D_STATE:(g + 1) * SSD_D_STATE]
        bt = bg.astype(F32).T
        cb = jnp.dot(cg, bt.astype(BF16), preferred_element_type=F32)
        st = state_ref[g]
        y_off = jnp.dot(cg, st.astype(BF16), preferred_element_type=F32)
        for j in range(SSD_HPG // 2):
            cols = slice(g * SSD_GROUP_W + j * LANES, g * SSD_GROUP_W + (j + 1) * LANES)
            tile = slice(j * LANES, (j + 1) * LANES)
            att, inj, scale = [], [], []
            for r in (2 * j, 2 * j + 1):
                h = dir_off + g * SSD_HPG + r
                ucol = jnp.broadcast_to(u2[:, h:h + 1], (q, q))
                decay_dt = jnp.exp2(jnp.where(allowed, ucol - ut_ref[h:h + 1, :], -jnp.inf))
                att.append((cb * decay_dt).astype(BF16))
                inj.append((bt * wt_ref[h:h + 1, :]).astype(BF16))
                scale.append(jnp.exp2(ucol))
            xt = xs_ref[:, cols]
            zt = jnp.zeros_like(xt)
            rhs = jnp.concatenate([jnp.where(low, xt, zt), jnp.where(low, zt, xt)], axis=0)
            lhs = jnp.concatenate([jnp.concatenate(att, axis=1), jnp.concatenate(inj, axis=1)], axis=0)
            res = jnp.dot(lhs, rhs, preferred_element_type=F32)
            y = res[:q] + y_off[:, tile] * jnp.where(low, scale[0], scale[1])
            if not reverse:
                y = y + d_ref[:, cols] * xt.astype(F32)
            y_ref[:, cols] = y.astype(y_ref.dtype)
            h = dir_off + g * SSD_HPG + 2 * j
            keep = jnp.where(low[0:1], state_decay[:, h:h + 1], state_decay[:, h + 1:h + 2])
            state_ref[g, :, tile] = st[:, tile] * keep + res[q:]


def _ssd_scan(xbc, dt_raw, dt_bias, a_log, d_row, seqs):
    t = xbc.shape[0]
    q = SSD_CHUNK
    nchunk = t // q
    starts = {r // q for r, _ in seqs}
    ends = {(r + s) // q - 1 for r, s in seqs}
    flags = np.array([(1 if c in starts else 0) | (2 if nchunk - 1 - c in ends else 0)
                      for c in range(nchunk)], np.int32)
    fwd = lambda c: c
    bwd = lambda c: nchunk - 1 - c
    nxb = SSD_D_INNER // SSD_BC_W

    def chunk_specs(blk):
        return [
            pl.BlockSpec((q, SSD_D_INNER), lambda c, f: (blk(c), 0)),
            pl.BlockSpec((q, SSD_BC_W), lambda c, f: (blk(c), nxb)),
            pl.BlockSpec((q, SSD_BC_W), lambda c, f: (blk(c), nxb + 1)),
            pl.BlockSpec((q, 2 * SSD_HEADS), lambda c, f: (blk(c), 0)),
        ]

    per_direction = [pltpu.VMEM((SSD_GROUPS, SSD_D_STATE, SSD_GROUP_W), F32),
                     pltpu.VMEM((2 * SSD_HEADS, q), F32),
                     pltpu.VMEM((2 * SSD_HEADS, q), F32)]
    grid_spec = pltpu.PrefetchScalarGridSpec(
        num_scalar_prefetch=1,
        grid=(nchunk,),
        in_specs=chunk_specs(fwd) + chunk_specs(bwd) + [
            pl.BlockSpec((1, 2 * SSD_HEADS), lambda c, f: (0, 0)),
            pl.BlockSpec((1, 2 * SSD_HEADS), lambda c, f: (0, 0)),
            pl.BlockSpec((1, SSD_D_INNER), lambda c, f: (0, 0)),
        ],
        out_specs=[pl.BlockSpec((q, SSD_D_INNER), lambda c, f: (fwd(c), 0)),
                   pl.BlockSpec((q, SSD_D_INNER), lambda c, f: (bwd(c), 0))],
        scratch_shapes=per_direction + per_direction,
    )
    out = jax.ShapeDtypeStruct((t, SSD_D_INNER), BF16)
    return pl.pallas_call(
        _scan_kernel,
        grid_spec=grid_spec,
        out_shape=[out, out],
        compiler_params=_cparams(1),
        name="ssd_scan",
    )(jnp.asarray(flags), xbc, xbc, xbc, dt_raw, xbc, xbc, xbc, dt_raw, dt_bias, a_log, d_row)


def _ssd_out_kernel(yf_ref, yb_ref, z_ref, nw_ref, w_ref, r_ref, o_ref):
    acc = r_ref[...]
    for g in range(SSD_GROUPS):
        cols = slice(g * SSD_GROUP_W, (g + 1) * SSD_GROUP_W)
        y = yf_ref[:, cols].astype(F32) + yb_ref[:, cols].astype(F32)
        gated = y * _silu(z_ref[:, cols].astype(F32))
        yn = _rms(gated, nw_ref[:, cols]).astype(BF16)
        acc = acc + jnp.dot(yn, w_ref[cols, :], preferred_element_type=F32)
    o_ref[...] = acc


def _ssd_out_proj(y_fwd, y_bwd, zx, nw, w, res):
    t, k = y_fwd.shape
    n = w.shape[1]
    bm = _pick(t, SSD_OUT_ROWS)
    return pl.pallas_call(
        _ssd_out_kernel,
        grid=(t // bm,),
        in_specs=[
            pl.BlockSpec((bm, k), lambda i: (i, 0)),
            pl.BlockSpec((bm, k), lambda i: (i, 0)),
            pl.BlockSpec((bm, k), lambda i: (i, 0)),
            pl.BlockSpec((1, k), lambda i: (0, 0)),
            pl.BlockSpec((k, n), lambda i: (0, 0), pipeline_mode=pl.Buffered(1)),
            pl.BlockSpec((bm, n), lambda i: (i, 0)),
        ],
        out_specs=pl.BlockSpec((bm, n), lambda i: (i, 0)),
        out_shape=jax.ShapeDtypeStruct((t, n), F32),
        compiler_params=_cparams(1),
        name="ssd_out_proj",
    )(y_fwd, y_bwd, zx, nw, w, res)


def _rope_tables(seqs, t):
    half = DA_HEAD_DIM // 2
    inv = 1.0 / (ROPE_THETA ** (jnp.arange(0, DA_HEAD_DIM, 2, dtype=F32) / DA_HEAD_DIM))
    pos = np.zeros((t,), np.float32)
    for r, s in seqs:
        pos[r:r + s] = np.arange(s, dtype=np.float32)
    ang = jnp.asarray(pos)[:, None] * inv[None, :]
    cos, sin = jnp.cos(ang), jnp.sin(ang)
    cos_t = jnp.concatenate([cos, cos, cos, cos], axis=1)
    sin_t = jnp.concatenate([-sin, -sin, sin, sin], axis=1)
    assert cos_t.shape == (t, LANES) and 4 * half == LANES
    return cos_t, sin_t


def _reorder_qk_columns(w_qkv):
    d = w_qkv.shape[0]
    half = DA_HEAD_DIM // 2
    qk = w_qkv[:, :2 * d].reshape(d, 2 * d // LANES, 2, 2, half)
    qk = qk.transpose(0, 1, 3, 2, 4).reshape(d, 2 * d)
    return jnp.concatenate([qk, w_qkv[:, 2 * d:]], axis=1)


def _seq_groups(seqs):
    groups = []
    for r, s in seqs:
        if groups and groups[-1][2] == s and groups[-1][0] + groups[-1][1] * s == r:
            groups[-1][1] += 1
        else:
            groups.append([r, 1, s])
    return [tuple(g) for g in groups]


def _attention_layer(x_parts, seqs, nw, w_qkv, w_o, lam_p, sub_w, lambda_init):
    t = sum(p.shape[0] for p in x_parts)
    cos_t, sin_t = _rope_tables(seqs, t)
    qkv = _qkv_proj(x_parts, nw, _reorder_qk_columns(w_qkv).astype(BF16), cos_t, sin_t)
    o = None
    for r, n, s in _seq_groups(seqs):
        o = _attention_group(qkv, lam_p, sub_w, r, n, s, lambda_init, o)
    return _matmul_res(o, w_o.astype(BF16), x_parts)


def _ssd_layer(x, seqs, nw, w_in, conv_w, conv_b, dt_bias, a_log, d_skip, norm_w, w_out):
    nzx = SSD_D_INNER + SSD_CONV_DIM
    zx, dt_raw = _ssd_in_proj(x, nw, w_in[:, :nzx].astype(BF16), w_in[:, nzx:].astype(BF16))
    xbc = _conv_silu(zx, conv_w, conv_b.reshape(1, -1), seqs, SSD_D_INNER)
    bias = dt_bias.reshape(1, -1)
    alog = a_log.reshape(1, -1)
    d_row = jnp.repeat(d_skip, SSD_HEAD_DIM).reshape(1, -1)
    y_fwd, y_bwd = _ssd_scan(xbc, dt_raw, bias, alog, d_row, seqs)
    return _ssd_out_proj(y_fwd, y_bwd, zx, norm_w.reshape(1, -1), w_out.astype(BF16), x)


def _trunk(x_parts, seqs, norm_mix, norm_ffn, norm_final, da_w_qkv, da_w_o, da_lambda_q1, da_lambda_k1,
           da_lambda_q2, da_lambda_k2, da_subln, ssd_w_in, ssd_conv_w, ssd_conv_b, ssd_dt_bias,
           ssd_a_log, ssd_d, ssd_norm, ssd_w_out, ffn_w_gu, ffn_w_down):
    depth = norm_mix.shape[0]
    nf = norm_final.reshape(1, -1)
    part_rows = [p.shape[0] for p in x_parts]
    x = x_parts
    for i in range(depth):
        j = i // 2
        nw = norm_mix[i].reshape(1, -1)
        if i % 2 == 0:
            lambda_init = 0.8 - 0.6 * math.exp(-0.3 * i)
            lam_p = jnp.stack([da_lambda_q1[j], da_lambda_k1[j], da_lambda_q2[j], da_lambda_k2[j]])
            y = _attention_layer(x, seqs, nw, da_w_qkv[j], da_w_o[j], lam_p,
                                 da_subln[j].reshape(1, -1), lambda_init)
        else:
            (x0,) = x
            y = _ssd_layer(x0, seqs, nw, ssd_w_in[j], ssd_conv_w[j], ssd_conv_b[j], ssd_dt_bias[j],
                           ssd_a_log[j], ssd_d[j], ssd_norm[j], ssd_w_out[j])
        last = i == depth - 1
        x = tuple(_ffn(y, norm_ffn[i].reshape(1, -1), ffn_w_gu[i].astype(BF16), ffn_w_down[i].astype(BF16),
                       nf, final_norm=last, out_rows=part_rows if last else [y.shape[0]]))
    return x


def kernel(x_prompt, x_sample, norm_mix, norm_ffn, norm_final, da_w_qkv, da_w_o, da_lambda_q1, da_lambda_k1, da_lambda_q2, da_lambda_k2, da_subln, ssd_w_in, ssd_conv_w, ssd_conv_b, ssd_dt_bias, ssd_a_log, ssd_d, ssd_norm, ssd_w_out, ffn_w_gu, ffn_w_down):
    d = x_prompt.shape[-1]
    seqs = []
    for arr in (x_prompt, x_sample):
        for _ in range(arr.shape[0]):
            seqs.append((sum(s for _, s in seqs), arr.shape[1]))
    x_parts = (x_prompt.reshape(-1, d), x_sample.reshape(-1, d))
    y_p, y_s = _trunk(x_parts, tuple(seqs), norm_mix, norm_ffn, norm_final, da_w_qkv, da_w_o,
                      da_lambda_q1, da_lambda_k1, da_lambda_q2, da_lambda_k2, da_subln, ssd_w_in,
                      ssd_conv_w, ssd_conv_b, ssd_dt_bias, ssd_a_log, ssd_d, ssd_norm, ssd_w_out,
                      ffn_w_gu, ffn_w_down)
    return y_p.reshape(x_prompt.shape), y_s.reshape(x_sample.shape)
```

```python
import functools
import math

import numpy as np
import jax
import jax.numpy as jnp
from jax import lax
from jax.experimental import pallas as pl
from jax.experimental.pallas import tpu as pltpu

F32 = jnp.float32
BF16 = jnp.bfloat16

EPS = 1e-5
LOG2E = 1.4426950408889634
LANES = 128
SUBLANES = 8
BF16_ROWS = 2 * SUBLANES
VMEM_LIMIT = 56 * 1024 * 1024

QKV_TILE = (512, 2048)
RES_ROWS = 512
FFN_TILE = (512, 512)
SSD_IN_TILE = (1024, 1024)
CONV_TILE = (512, 1024)
SSD_OUT_ROWS = 256
ATTN_TQ = 512
ATTN_TK = 1024
ATTN_MAX_UNROLLED_STAGES = 8

DA_HEADS = 16
DA_HEAD_DIM = 64
DA_V_DIM = 2 * DA_HEAD_DIM
ROPE_THETA = 10000.0

SSD_HEAD_DIM = 64
SSD_GROUPS = 8
SSD_HPG = 8
SSD_HEADS = SSD_GROUPS * SSD_HPG
SSD_D_STATE = 128
SSD_D_INNER = SSD_HEADS * SSD_HEAD_DIM
SSD_CHUNK = 128
SSD_GROUP_W = SSD_HPG * SSD_HEAD_DIM
SSD_BC_W = SSD_GROUPS * SSD_D_STATE
SSD_CONV_DIM = SSD_D_INNER + 2 * SSD_BC_W


def _cparams(n_axes):
    return pltpu.CompilerParams(dimension_semantics=("arbitrary",) * n_axes,
                                vmem_limit_bytes=VMEM_LIMIT)


def _rms(x, w):
    ms = jnp.mean(x * x, axis=-1, keepdims=True)
    return x * lax.rsqrt(ms + EPS) * w


def _silu(x):
    return x / (1.0 + jnp.exp(-x))


def _pick(n, pref):
    b = min(n, pref)
    while n % b:
        b //= 2
    return b


def _part_offsets(parts, bm):
    offs = [0]
    for p in parts:
        assert p.shape[0] % bm == 0
        offs.append(offs[-1] + p.shape[0] // bm)
    return offs


def _part_spec(block, offs, p, col_fn):
    lo, n = offs[p], offs[p + 1] - offs[p]
    return pl.BlockSpec(block, lambda i, j: (jnp.clip(i - lo, 0, n - 1), col_fn(i, j, lo, n)))


def _in_part(i, offs, p):
    return jnp.logical_and(i >= offs[p], i < offs[p + 1])


def _qkv_kernel(*refs, offs, n_q_tiles, n_rope_tiles):
    nparts = len(offs) - 1
    x_refs = refs[:nparts]
    nw_ref, w_ref, cos_ref, sin_ref, o_ref, h_ref = refs[nparts:]
    i = pl.program_id(0)
    j = pl.program_id(1)

    for p in range(nparts):
        @pl.when(jnp.logical_and(j == 0, _in_part(i, offs, p)))
        def _(p=p):
            h_ref[...] = _rms(x_refs[p][...], nw_ref[...]).astype(BF16)

    y = jnp.dot(h_ref[...], w_ref[...], preferred_element_type=F32)
    rope = j < n_rope_tiles
    c = jnp.where(rope, cos_ref[...], 1.0)
    s = jnp.where(rope, sin_ref[...], 0.0)
    scale = jnp.where(j < n_q_tiles, LOG2E * DA_HEAD_DIM ** -0.5, 1.0)
    c = c * scale
    s = s * scale
    for t in range(y.shape[1] // LANES):
        yt = y[:, t * LANES:(t + 1) * LANES]
        o_ref[:, t * LANES:(t + 1) * LANES] = (yt * c + pltpu.roll(yt, LANES // 2, 1) * s).astype(o_ref.dtype)


def _qkv_proj(x_parts, nw, w, cos_t, sin_t):
    d = x_parts[0].shape[1]
    t = sum(p.shape[0] for p in x_parts)
    n = w.shape[1]
    bm = _pick(math.gcd(*[p.shape[0] for p in x_parts]), QKV_TILE[0])
    bn = _pick(d, QKV_TILE[1])
    offs = _part_offsets(x_parts, bm)
    return pl.pallas_call(
        functools.partial(_qkv_kernel, offs=tuple(offs), n_q_tiles=d // bn, n_rope_tiles=2 * d // bn),
        grid=(t // bm, n // bn),
        in_specs=[_part_spec((bm, d), offs, p, lambda i, j, lo, cnt: 0) for p in range(len(x_parts))] + [
            pl.BlockSpec((1, d), lambda i, j: (0, 0)),
            pl.BlockSpec((d, bn), lambda i, j: (0, j)),
            pl.BlockSpec((bm, LANES), lambda i, j: (i, 0)),
            pl.BlockSpec((bm, LANES), lambda i, j: (i, 0)),
        ],
        out_specs=pl.BlockSpec((bm, bn), lambda i, j: (i, j)),
        out_shape=jax.ShapeDtypeStruct((t, n), BF16),
        scratch_shapes=[pltpu.VMEM((bm, d), BF16)],
        compiler_params=_cparams(2),
        name="qkv_proj",
    )(*x_parts, nw, w, cos_t, sin_t)


def _attn_kernel(lam_ref, sub_ref, q_ref, k_ref, v_ref, o_ref, vt_ref, qt_ref, s_ref, smax_ref,
                 m_ref, acc_ref, *, tk, tq, lambda_init):
    n_sub = q_ref.shape[0] // tq
    nk = k_ref.shape[0] // tk

    @pl.when(pl.program_id(2) == 0)
    def _():
        def transpose_chunk(c, carry):
            start = pl.multiple_of(c * tk, tk)
            vt_ref[0:DA_V_DIM, pl.ds(start, tk)] = v_ref[pl.ds(start, tk), :].astype(F32).T.astype(BF16)
            return carry
        lax.fori_loop(0, nk, transpose_chunk, 0)
        extra = lax.broadcasted_iota(jnp.int32, (vt_ref.shape[0] - DA_V_DIM, vt_ref.shape[1]), 0)
        vt_ref[DA_V_DIM:, :] = jnp.where(extra == 0, 1.0, 0.0).astype(BF16)

    row = lax.broadcasted_iota(jnp.int32, (DA_V_DIM, tq), 0)
    in_map0 = (row % DA_HEAD_DIM) < (DA_HEAD_DIM // 2)
    for b in range(n_sub):
        qt = q_ref[b * tq:(b + 1) * tq, :].astype(F32).T
        qt_ref[b, 0] = jnp.where(in_map0, qt, 0.0).astype(BF16)
        qt_ref[b, 1] = jnp.where(in_map0, 0.0, qt).astype(BF16)
    m_ref[...] = jnp.full(m_ref.shape, -jnp.inf, F32)
    acc_ref[...] = jnp.zeros(acc_ref.shape, F32)

    def tile_start(tile):
        return tile * tk if isinstance(tile, int) else pl.multiple_of(tile * tk, tk)

    def scores(b, tile, slot):
        k = k_ref[pl.ds(tile_start(tile), tk), :]
        for mp in range(2):
            st = jnp.dot(k, qt_ref[b, mp], preferred_element_type=F32)
            s_ref[slot, mp] = st
            smax_ref[slot, mp] = jnp.max(st, axis=0, keepdims=True)

    def update(b, tile, slot):
        vt = vt_ref[:, pl.ds(tile_start(tile), tk)]
        for mp in range(2):
            m_old = m_ref[b, mp]
            m_new = jnp.maximum(m_old, smax_ref[slot, mp])
            alpha = jnp.exp2(m_old - m_new)
            p = jnp.exp2(s_ref[slot, mp] - m_new)
            acc_ref[b, mp] = alpha * acc_ref[b, mp] + jnp.dot(vt, p.astype(BF16), preferred_element_type=F32)
            m_ref[b, mp] = m_new

    n_stages = n_sub * nk
    if n_stages > ATTN_MAX_UNROLLED_STAGES:
        def at(u):
            return (u // nk, u % nk) if isinstance(u, int) else (lax.div(u, nk), lax.rem(u, nk))

        scores(0, 0, 0)

        def stage_pair(j):
            u = 2 * j
            scores(*at(u + 1), 1)
            update(*at(u), 0)
            scores(*at(u + 2), 0)
            update(*at(u + 1), 1)

        def body(i, carry):
            stage_pair(2 * i)
            stage_pair(2 * i + 1)
            return carry

        n_pairs = n_stages // 2 - 1
        lax.fori_loop(0, n_pairs // 2, body, 0)
        if n_pairs % 2:
            stage_pair(n_pairs - 1)
        scores(*at(n_stages - 1), 1)
        update(*at(n_stages - 2), 0)
        update(*at(n_stages - 1), 1)
    else:
        stages = [(b, t) for b in range(n_sub) for t in range(nk)]
        scores(*stages[0], 0)
        for i in range(len(stages) - 1):
            scores(*stages[i + 1], (i + 1) % 2)
            update(*stages[i], i % 2)
        update(*stages[-1], (len(stages) - 1) % 2)

    lam_p = lam_ref[...]
    t1 = jnp.sum(lam_p[0:1, :] * lam_p[1:2, :], axis=-1, keepdims=True)
    t2 = jnp.sum(lam_p[2:3, :] * lam_p[3:4, :], axis=-1, keepdims=True)
    lam = jnp.exp(t1) - jnp.exp(t2) + lambda_init
    for b in range(n_sub):
        acc0, acc1 = acc_ref[b, 0], acc_ref[b, 1]
        ot = (acc0[:DA_V_DIM] / acc0[DA_V_DIM:DA_V_DIM + 1]
              - lam * (acc1[:DA_V_DIM] / acc1[DA_V_DIM:DA_V_DIM + 1]))
        ms = jnp.mean(ot * ot, axis=0, keepdims=True)
        o = (ot * lax.rsqrt(ms + EPS)).T
        o_ref[b * tq:(b + 1) * tq, :] = (o * sub_ref[...] * (1.0 - lambda_init)).astype(o_ref.dtype)


def _attn_alias_kernel(lam_ref, sub_ref, q_ref, k_ref, v_ref, prev_ref, o_ref, *scratch, **kw):
    del prev_ref
    _attn_kernel(lam_ref, sub_ref, q_ref, k_ref, v_ref, o_ref, *scratch, **kw)


def _attention_group(qkv, lam_p, sub_w, row0, nseq, s, lambda_init, prev_out):
    tq = _pick(s, ATTN_TQ)
    tk = _pick(s // 2, ATTN_TK)
    if (s // tq) * (s // tk) <= ATTN_MAX_UNROLLED_STAGES:
        n_sub = s // tq
    else:
        tq, n_sub = tq // 2, 2
    bq = n_sub * tq
    nq = s // bq
    assert row0 % s == 0 and (s // tk) % 2 == 0
    kw = dict(tk=tk, tq=tq, lambda_init=lambda_init)
    in_specs = [
        pl.BlockSpec((4, DA_HEAD_DIM), lambda b, h, i: (0, 0)),
        pl.BlockSpec((1, DA_V_DIM), lambda b, h, i: (0, 0)),
        pl.BlockSpec((bq, DA_V_DIM), lambda b, h, i: (row0 // bq + b * nq + i, h)),
        pl.BlockSpec((s, DA_V_DIM), lambda b, h, i: (row0 // s + b, DA_HEADS + h)),
        pl.BlockSpec((s, DA_V_DIM), lambda b, h, i: (row0 // s + b, 2 * DA_HEADS + h)),
    ]
    args = [lam_p, sub_w, qkv, qkv, qkv]
    if prev_out is None:
        kern, aliases = functools.partial(_attn_kernel, **kw), {}
    else:
        kern, aliases = functools.partial(_attn_alias_kernel, **kw), {len(args): 0}
        in_specs.append(pl.BlockSpec(memory_space=pl.ANY))
        args.append(prev_out)
    return pl.pallas_call(
        kern,
        grid=(nseq, DA_HEADS, nq),
        in_specs=in_specs,
        out_specs=pl.BlockSpec((bq, DA_V_DIM), lambda b, h, i: (row0 // bq + b * nq + i, h)),
        out_shape=jax.ShapeDtypeStruct((qkv.shape[0], DA_HEADS * DA_V_DIM), BF16),
        input_output_aliases=aliases,
        scratch_shapes=[
            pltpu.VMEM((DA_V_DIM + BF16_ROWS, s), BF16),
            pltpu.VMEM((n_sub, 2, DA_V_DIM, tq), BF16),
            pltpu.VMEM((2, 2, tk, tq), F32),
            pltpu.VMEM((2, 2, 1, tq), F32),
            pltpu.VMEM((n_sub, 2, 1, tq), F32),
            pltpu.VMEM((n_sub, 2, DA_V_DIM + BF16_ROWS, tq), F32),
        ],
        compiler_params=_cparams(3),
        name="diff_attention",
    )(*args)


def _matmul_res_kernel(a_ref, w_ref, *refs, offs):
    r_refs, o_ref = refs[:-1], refs[-1]
    i = pl.program_id(0)
    y = jnp.dot(a_ref[...], w_ref[...], preferred_element_type=F32)
    for p, r_ref in enumerate(r_refs):
        @pl.when(_in_part(i, offs, p))
        def _(r_ref=r_ref):
            o_ref[...] = r_ref[...] + y


def _matmul_res(a, w, res_parts):
    t, k = a.shape
    n = w.shape[1]
    bm = _pick(math.gcd(*[p.shape[0] for p in res_parts]), RES_ROWS)
    offs = _part_offsets(res_parts, bm)

    def part_spec(p):
        lo, cnt = offs[p], offs[p + 1] - offs[p]
        return pl.BlockSpec((bm, n), lambda i: (jnp.clip(i - lo, 0, cnt - 1), 0))

    return pl.pallas_call(
        functools.partial(_matmul_res_kernel, offs=tuple(offs)),
        grid=(t // bm,),
        in_specs=[
            pl.BlockSpec((bm, k), lambda i: (i, 0)),
            pl.BlockSpec((k, n), lambda i: (0, 0), pipeline_mode=pl.Buffered(1)),
        ] + [part_spec(p) for p in range(len(res_parts))],
        out_specs=pl.BlockSpec((bm, n), lambda i: (i, 0)),
        out_shape=jax.ShapeDtypeStruct((t, n), F32),
        compiler_params=_cparams(1),
        name="matmul_residual",
    )(a, w, *res_parts)


def _ffn_kernel(x_ref, nw_ref, wg_ref, wu_ref, wd_ref, nf_ref, *refs, offs, final_norm):
    o_refs, (h_ref, acc_ref) = refs[:-2], refs[-2:]
    i = pl.program_id(0)
    j = pl.program_id(1)

    @pl.when(j == 0)
    def _():
        h_ref[...] = _rms(x_ref[...], nw_ref[...]).astype(BF16)
        acc_ref[...] = jnp.zeros_like(acc_ref)

    h = h_ref[...]
    g = jnp.dot(h, wg_ref[...], preferred_element_type=F32)
    u = jnp.dot(h, wu_ref[...], preferred_element_type=F32)
    a = (_silu(g) * u).astype(BF16)
    acc_ref[...] += jnp.dot(a, wd_ref[...], preferred_element_type=F32)

    last = j == pl.num_programs(1) - 1
    for p, o_ref in enumerate(o_refs):
        @pl.when(jnp.logical_and(last, _in_part(i, offs, p)))
        def _(o_ref=o_ref):
            r = acc_ref[...] + x_ref[...]
            if final_norm:
                r = _rms(r, nf_ref[...])
            o_ref[...] = r


def _ffn(x, nw, w_gu, w_down, nf, final_norm, out_rows):
    t, d = x.shape
    dff = w_down.shape[0]
    bm = _pick(math.gcd(*out_rows), FFN_TILE[0])
    bf = _pick(dff, FFN_TILE[1])
    nf_tiles = dff // bf
    outs = [jax.ShapeDtypeStruct((r, d), F32) for r in out_rows]
    offs = _part_offsets(outs, bm)
    return pl.pallas_call(
        functools.partial(_ffn_kernel, offs=tuple(offs), final_norm=final_norm),
        grid=(t // bm, nf_tiles),
        in_specs=[
            pl.BlockSpec((bm, d), lambda i, j: (i, 0)),
            pl.BlockSpec((1, d), lambda i, j: (0, 0)),
            pl.BlockSpec((d, bf), lambda i, j: (0, j)),
            pl.BlockSpec((d, bf), lambda i, j: (0, j + nf_tiles)),
            pl.BlockSpec((bf, d), lambda i, j: (j, 0)),
            pl.BlockSpec((1, d), lambda i, j: (0, 0)),
        ],
        out_specs=[_part_spec((bm, d), offs, p, lambda i, j, lo, cnt: 0) for p in range(len(outs))],
        out_shape=outs,
        scratch_shapes=[pltpu.VMEM((bm, d), BF16), pltpu.VMEM((bm, d), F32)],
        compiler_params=_cparams(2),
        name="swiglu_ffn",
    )(x, nw, w_gu, w_gu, w_down, nf)


def _ssd_in_kernel(x_ref, nw_ref, w_ref, wdt_ref, o_ref, dt_ref, h_ref):
    j = pl.program_id(1)

    @pl.when(j == 0)
    def _():
        h = _rms(x_ref[...], nw_ref[...]).astype(BF16)
        h_ref[...] = h
        dt_ref[...] = jnp.dot(h, wdt_ref[...], preferred_element_type=F32)

    o_ref[...] = jnp.dot(h_ref[...], w_ref[...], preferred_element_type=F32).astype(o_ref.dtype)


def _ssd_in_proj(x, nw, w_zx, w_dt):
    t, d = x.shape
    n = w_zx.shape[1]
    ndt = w_dt.shape[1]
    bm = _pick(t, SSD_IN_TILE[0])
    bn = _pick(n, SSD_IN_TILE[1])
    return pl.pallas_call(
        _ssd_in_kernel,
        grid=(t // bm, n // bn),
        in_specs=[
            pl.BlockSpec((bm, d), lambda i, j: (i, 0)),
            pl.BlockSpec((1, d), lambda i, j: (0, 0)),
            pl.BlockSpec((d, bn), lambda i, j: (0, j)),
            pl.BlockSpec((d, ndt), lambda i, j: (0, 0)),
        ],
        out_specs=[
            pl.BlockSpec((bm, bn), lambda i, j: (i, j)),
            pl.BlockSpec((bm, ndt), lambda i, j: (i, 0)),
        ],
        out_shape=[jax.ShapeDtypeStruct((t, n), BF16), jax.ShapeDtypeStruct((t, ndt), F32)],
        scratch_shapes=[pltpu.VMEM((bm, d), BF16)],
        compiler_params=_cparams(2),
        name="ssd_in_proj",
    )(x, nw, w_zx, w_dt)


def _conv_kernel(flags_ref, prev_ref, cur_ref, next_ref, w_ref, b_ref, o_ref):
    f = flags_ref[pl.program_id(0)]
    at_start = (f & 1) == 1
    at_end = (f & 2) == 2
    x = cur_ref[...].astype(F32)
    bt = x.shape[0]
    row = lax.broadcasted_iota(jnp.int32, x.shape, 0)
    before = jnp.where(at_start, 0.0, prev_ref[SUBLANES - 1:SUBLANES, :].astype(F32))
    after = jnp.where(at_end, 0.0, next_ref[0:2, :].astype(F32))
    xm1 = jnp.where(row == 0, before, pltpu.roll(x, 1, 0))
    xp1 = jnp.where(row == bt - 1, after[0:1], pltpu.roll(x, bt - 1, 0))
    xp2 = jnp.where(row == bt - 2, after[0:1],
                    jnp.where(row == bt - 1, after[1:2], pltpu.roll(x, bt - 2, 0)))
    w = w_ref[...]
    y = xm1 * w[0:1] + x * w[1:2] + xp1 * w[2:3] + xp2 * w[3:4] + b_ref[...]
    o_ref[...] = _silu(y).astype(o_ref.dtype)


def _conv_silu(zx, conv_w, conv_b, seqs, col0):
    t = zx.shape[0]
    ncol = conv_w.shape[1]
    bt = _pick(math.gcd(*[s for _, s in seqs]), CONV_TILE[0])
    bc = _pick(ncol, CONV_TILE[1])
    nblk = t // bt
    starts = {r for r, _ in seqs}
    ends = {r + s for r, s in seqs}
    flags = np.array([(1 if i * bt in starts else 0) | (2 if (i + 1) * bt in ends else 0)
                      for i in range(nblk)], np.int32)
    hb = bt // SUBLANES
    c0 = col0 // bc
    grid_spec = pltpu.PrefetchScalarGridSpec(
        num_scalar_prefetch=1,
        grid=(nblk, ncol // bc),
        in_specs=[
            pl.BlockSpec((SUBLANES, bc), lambda i, j, f: (jnp.maximum(i * hb - 1, 0), c0 + j)),
            pl.BlockSpec((bt, bc), lambda i, j, f: (i, c0 + j)),
            pl.BlockSpec((SUBLANES, bc), lambda i, j, f: (jnp.minimum((i + 1) * hb, t // SUBLANES - 1), c0 + j)),
            pl.BlockSpec((4, bc), lambda i, j, f: (0, j)),
            pl.BlockSpec((1, bc), lambda i, j, f: (0, j)),
        ],
        out_specs=pl.BlockSpec((bt, bc), lambda i, j, f: (i, j)),
    )
    return pl.pallas_call(
        _conv_kernel,
        grid_spec=grid_spec,
        out_shape=jax.ShapeDtypeStruct((t, ncol), BF16),
        compiler_params=_cparams(2),
        name="ssd_conv_silu",
    )(jnp.asarray(flags), zx, zx, zx, conv_w, conv_b)


def _scan_kernel(flags_ref, xs_f, b_f, c_f, dt_f, xs_b, b_b, c_b, dt_b, bias_ref, alog_ref, d_ref,
                 yf_ref, yb_ref, st_f, ut_f, wt_f, st_b, ut_b, wt_b):
    f = flags_ref[pl.program_id(0)]

    @pl.when((f & 1) == 1)
    def _():
        st_f[...] = jnp.zeros_like(st_f)

    @pl.when((f & 2) == 2)
    def _():
        st_b[...] = jnp.zeros_like(st_b)

    _scan_direction(xs_f, b_f, c_f, dt_f, bias_ref, alog_ref, d_ref, yf_ref, st_f, ut_f, wt_f, reverse=False)
    _scan_direction(xs_b, b_b, c_b, dt_b, bias_ref, alog_ref, d_ref, yb_ref, st_b, ut_b, wt_b, reverse=True)


def _scan_direction(xs_ref, b_ref, c_ref, dt_ref, bias_ref, alog_ref, d_ref, y_ref, state_ref, ut_ref,
                    wt_ref, *, reverse):
    q = SSD_CHUNK
    dir_off = SSD_HEADS if reverse else 0
    xdt_in = dt_ref[...] + bias_ref[...]
    dt = jnp.maximum(xdt_in, 0.0) + jnp.log(1.0 + jnp.exp(-jnp.abs(xdt_in)))
    da = dt * (-jnp.exp(alog_ref[...]))
    row = lax.broadcasted_iota(jnp.int32, (q, LANES), 0)
    cum = da
    k = 1
    while k < q:
        cum = cum + jnp.where(row >= k, pltpu.roll(cum, k, 0), 0.0)
        k *= 2
    tot = cum[q - 1:q, :]
    u = (tot + da - cum) if reverse else cum
    state_decay = jnp.exp(tot)
    u2 = u * LOG2E
    ut_ref[...] = (u2 - jnp.log2(dt)).T
    wt_ref[...] = (dt * jnp.exp(tot - u)).T

    li = lax.broadcasted_iota(jnp.int32, (q, q), 0)
    si = lax.broadcasted_iota(jnp.int32, (q, q), 1)
    allowed = (si >= li) if reverse else (li >= si)
    low = lax.broadcasted_iota(jnp.int32, (q, LANES), 1) < SSD_HEAD_DIM

    for g in range(SSD_GROUPS):
        bg = b_ref[:, g * SSD_D_STATE:(g + 1) * SSD_D_STATE]
        cg = c_ref[:, g * SSD_D_STATE:(g + 1) * SSD_D_STATE]
        bt = bg.astype(F32).T
        cb = jnp.dot(cg, bt.astype(BF16), preferred_element_type=F32)
        st = state_ref[g]
        y_off = jnp.dot(cg, st.astype(BF16), preferred_element_type=F32)
        for j in range(SSD_HPG // 2):
            cols = slice(g * SSD_GROUP_W + j * LANES, g * SSD_GROUP_W + (j + 1) * LANES)
            tile = slice(j * LANES, (j + 1) * LANES)
            att, inj, scale = [], [], []
            for r in (2 * j, 2 * j + 1):
                h = dir_off + g * SSD_HPG + r
                ucol = jnp.broadcast_to(u2[:, h:h + 1], (q, q))
                decay_dt = jnp.exp2(jnp.where(allowed, ucol - ut_ref[h:h + 1, :], -jnp.inf))
                att.append((cb * decay_dt).astype(BF16))
                inj.append((bt * wt_ref[h:h + 1, :]).astype(BF16))
                scale.append(jnp.exp2(ucol))
            xt = xs_ref[:, cols]
            zt = jnp.zeros_like(xt)
            rhs = jnp.concatenate([jnp.where(low, xt, zt), jnp.where(low, zt, xt)], axis=0)
            lhs = jnp.concatenate([jnp.concatenate(att, axis=1), jnp.concatenate(inj, axis=1)], axis=0)
            res = jnp.dot(lhs, rhs, preferred_element_type=F32)
            y = res[:q] + y_off[:, tile] * jnp.where(low, scale[0], scale[1])
            if not reverse:
                y = y + d_ref[:, cols] * xt.astype(F32)
            y_ref[:, cols] = y.astype(y_ref.dtype)
            h = dir_off + g * SSD_HPG + 2 * j
            keep = jnp.where(low[0:1], state_decay[:, h:h + 1], state_decay[:, h + 1:h + 2])
            state_ref[g, :, tile] = st[:, tile] * keep + res[q:]


def _ssd_scan(xbc, dt_raw, dt_bias, a_log, d_row, seqs):
    t = xbc.shape[0]
    q = SSD_CHUNK
    nchunk = t // q
    starts = {r // q for r, _ in seqs}
    ends = {(r + s) // q - 1 for r, s in seqs}
    flags = np.array([(1 if c in starts else 0) | (2 if nchunk - 1 - c in ends else 0)
                      for c in range(nchunk)], np.int32)
    fwd = lambda c: c
    bwd = lambda c: nchunk - 1 - c
    nxb = SSD_D_INNER // SSD_BC_W

    def chunk_specs(blk):
        return [
            pl.BlockSpec((q, SSD_D_INNER), lambda c, f: (blk(c), 0)),
            pl.BlockSpec((q, SSD_BC_W), lambda c, f: (blk(c), nxb)),
            pl.BlockSpec((q, SSD_BC_W), lambda c, f: (blk(c), nxb + 1)),
            pl.BlockSpec((q, 2 * SSD_HEADS), lambda c, f: (blk(c), 0)),
        ]

    per_direction = [pltpu.VMEM((SSD_GROUPS, SSD_D_STATE, SSD_GROUP_W), F32),
                     pltpu.VMEM((2 * SSD_HEADS, q), F32),
                     pltpu.VMEM((2 * SSD_HEADS, q), F32)]
    grid_spec = pltpu.PrefetchScalarGridSpec(
        num_scalar_prefetch=1,
        grid=(nchunk,),
        in_specs=chunk_specs(fwd) + chunk_specs(bwd) + [
            pl.BlockSpec((1, 2 * SSD_HEADS), lambda c, f: (0, 0)),
            pl.BlockSpec((1, 2 * SSD_HEADS), lambda c, f: (0, 0)),
            pl.BlockSpec((1, SSD_D_INNER), lambda c, f: (0, 0)),
        ],
        out_specs=[pl.BlockSpec((q, SSD_D_INNER), lambda c, f: (fwd(c), 0)),
                   pl.BlockSpec((q, SSD_D_INNER), lambda c, f: (bwd(c), 0))],
        scratch_shapes=per_direction + per_direction,
    )
    out = jax.ShapeDtypeStruct((t, SSD_D_INNER), BF16)
    return pl.pallas_call(
        _scan_kernel,
        grid_spec=grid_spec,
        out_shape=[out, out],
        compiler_params=_cparams(1),
        name="ssd_scan",
    )(jnp.asarray(flags), xbc, xbc, xbc, dt_raw, xbc, xbc, xbc, dt_raw, dt_bias, a_log, d_row)


def _ssd_out_kernel(yf_ref, yb_ref, z_ref, nw_ref, w_ref, r_ref, o_ref):
    acc = r_ref[...]
    for g in range(SSD_GROUPS):
        cols = slice(g * SSD_GROUP_W, (g + 1) * SSD_GROUP_W)
        y = yf_ref[:, cols].astype(F32) + yb_ref[:, cols].astype(F32)
        gated = y * _silu(z_ref[:, cols].astype(F32))
        yn = _rms(gated, nw_ref[:, cols]).astype(BF16)
        acc = acc + jnp.dot(yn, w_ref[cols, :], preferred_element_type=F32)
    o_ref[...] = acc


def _ssd_out_proj(y_fwd, y_bwd, zx, nw, w, res):
    t, k = y_fwd.shape
    n = w.shape[1]
    bm = _pick(t, SSD_OUT_ROWS)
    return pl.pallas_call(
        _ssd_out_kernel,
        grid=(t // bm,),
        in_specs=[
            pl.BlockSpec((bm, k), lambda i: (i, 0)),
            pl.BlockSpec((bm, k), lambda i: (i, 0)),
            pl.BlockSpec((bm, k), lambda i: (i, 0)),
            pl.BlockSpec((1, k), lambda i: (0, 0)),
            pl.BlockSpec((k, n), lambda i: (0, 0), pipeline_mode=pl.Buffered(1)),
            pl.BlockSpec((bm, n), lambda i: (i, 0)),
        ],
        out_specs=pl.BlockSpec((bm, n), lambda i: (i, 0)),
        out_shape=jax.ShapeDtypeStruct((t, n), F32),
        compiler_params=_cparams(1),
        name="ssd_out_proj",
    )(y_fwd, y_bwd, zx, nw, w, res)


def _rope_tables(seqs, t):
    half = DA_HEAD_DIM // 2
    inv = 1.0 / (ROPE_THETA ** (jnp.arange(0, DA_HEAD_DIM, 2, dtype=F32) / DA_HEAD_DIM))
    pos = np.zeros((t,), np.float32)
    for r, s in seqs:
        pos[r:r + s] = np.arange(s, dtype=np.float32)
    ang = jnp.asarray(pos)[:, None] * inv[None, :]
    cos, sin = jnp.cos(ang), jnp.sin(ang)
    cos_t = jnp.concatenate([cos, cos, cos, cos], axis=1)
    sin_t = jnp.concatenate([-sin, -sin, sin, sin], axis=1)
    assert cos_t.shape == (t, LANES) and 4 * half == LANES
    return cos_t, sin_t


def _reorder_qk_columns(w_qkv):
    d = w_qkv.shape[0]
    half = DA_HEAD_DIM // 2
    qk = w_qkv[:, :2 * d].reshape(d, 2 * d // LANES, 2, 2, half)
    qk = qk.transpose(0, 1, 3, 2, 4).reshape(d, 2 * d)
    return jnp.concatenate([qk, w_qkv[:, 2 * d:]], axis=1)


def _seq_groups(seqs):
    groups = []
    for r, s in seqs:
        if groups and groups[-1][2] == s and groups[-1][0] + groups[-1][1] * s == r:
            groups[-1][1] += 1
        else:
            groups.append([r, 1, s])
    return [tuple(g) for g in groups]


def _attention_layer(x_parts, seqs, nw, w_qkv, w_o, lam_p, sub_w, lambda_init):
    t = sum(p.shape[0] for p in x_parts)
    cos_t, sin_t = _rope_tables(seqs, t)
    qkv = _qkv_proj(x_parts, nw, _reorder_qk_columns(w_qkv).astype(BF16), cos_t, sin_t)
    o = None
    for r, n, s in _seq_groups(seqs):
        o = _attention_group(qkv, lam_p, sub_w, r, n, s, lambda_init, o)
    return _matmul_res(o, w_o.astype(BF16), x_parts)


def _ssd_layer(x, seqs, nw, w_in, conv_w, conv_b, dt_bias, a_log, d_skip, norm_w, w_out):
    nzx = SSD_D_INNER + SSD_CONV_DIM
    zx, dt_raw = _ssd_in_proj(x, nw, w_in[:, :nzx].astype(BF16), w_in[:, nzx:].astype(BF16))
    xbc = _conv_silu(zx, conv_w, conv_b.reshape(1, -1), seqs, SSD_D_INNER)
    bias = dt_bias.reshape(1, -1)
    alog = a_log.reshape(1, -1)
    d_row = jnp.repeat(d_skip, SSD_HEAD_DIM).reshape(1, -1)
    y_fwd, y_bwd = _ssd_scan(xbc, dt_raw, bias, alog, d_row, seqs)
    return _ssd_out_proj(y_fwd, y_bwd, zx, norm_w.reshape(1, -1), w_out.astype(BF16), x)


def _trunk(x_parts, seqs, norm_mix, norm_ffn, norm_final, da_w_qkv, da_w_o, da_lambda_q1, da_lambda_k1,
           da_lambda_q2, da_lambda_k2, da_subln, ssd_w_in, ssd_conv_w, ssd_conv_b, ssd_dt_bias,
           ssd_a_log, ssd_d, ssd_norm, ssd_w_out, ffn_w_gu, ffn_w_down):
    depth = norm_mix.shape[0]
    nf = norm_final.reshape(1, -1)
    part_rows = [p.shape[0] for p in x_parts]
    x = x_parts
    for i in range(depth):
        j = i // 2
        nw = norm_mix[i].reshape(1, -1)
        if i % 2 == 0:
            lambda_init = 0.8 - 0.6 * math.exp(-0.3 * i)
            lam_p = jnp.stack([da_lambda_q1[j], da_lambda_k1[j], da_lambda_q2[j], da_lambda_k2[j]])
            y = _attention_layer(x, seqs, nw, da_w_qkv[j], da_w_o[j], lam_p,
                                 da_subln[j].reshape(1, -1), lambda_init)
        else:
            (x0,) = x
            y = _ssd_layer(x0, seqs, nw, ssd_w_in[j], ssd_conv_w[j], ssd_conv_b[j], ssd_dt_bias[j],
                           ssd_a_log[j], ssd_d[j], ssd_norm[j], ssd_w_out[j])
        last = i == depth - 1
        x = tuple(_ffn(y, norm_ffn[i].reshape(1, -1), ffn_w_gu[i].astype(BF16), ffn_w_down[i].astype(BF16),
                       nf, final_norm=last, out_rows=part_rows if last else [y.shape[0]]))
    return x


def kernel(x_prompt, x_sample, norm_mix, norm_ffn, norm_final, da_w_qkv, da_w_o, da_lambda_q1, da_lambda_k1, da_lambda_q2, da_lambda_k2, da_subln, ssd_w_in, ssd_conv_w, ssd_conv_b, ssd_dt_bias, ssd_a_log, ssd_d, ssd_norm, ssd_w_out, ffn_w_gu, ffn_w_down):
    d = x_prompt.shape[-1]
    seqs = []
    for arr in (x_prompt, x_sample):
        for _ in range(arr.shape[0]):
            seqs.append((sum(s for _, s in seqs), arr.shape[1]))
    x_parts = (x_prompt.reshape(-1, d), x_sample.reshape(-1, d))
    y_p, y_s = _trunk(x_parts, tuple(seqs), norm_mix, norm_ffn, norm_final, da_w_qkv, da_w_o,
                      da_lambda_q1, da_lambda_k1, da_lambda_q2, da_lambda_k2, da_subln, ssd_w_in,
                      ssd_conv_w, ssd_conv_b, ssd_dt_bias, ssd_a_log, ssd_d, ssd_norm, ssd_w_out,
                      ffn_w_gu, ffn_w_down)
    return y_p.reshape(x_prompt.shape), y_s.reshape(x_sample.shape)
```
